```python
import math
import jax, jax.numpy as jnp
from jax import lax
import numpy as np

D_MODEL = 1024
BATCH = 8
SEQ = 2048
DEPTH = 2

D_S5 = 256
S5_GROUP = 16
S5_GROUPS = D_S5 // S5_GROUP
S5_STATE = 64
D_RG = 256
RG_BLOCKS = 8
RG_BLOCK = D_RG // RG_BLOCKS
RG_CONV = 4
RG_C = 8.0
N_HEADS = 8
N_KV_HEADS = 2
HEAD_DIM = 64
D_ATTN = N_HEADS * HEAD_DIM
KV_DIM = N_KV_HEADS * HEAD_DIM
Q_PER_KV = N_HEADS // N_KV_HEADS
IDX_HEADS = 8
IDX_DIM = 64
TOPK_MAX = 256
Q_BLOCK = 128
D_MIX = D_S5 + D_RG + D_ATTN
ROPE_THETA = 10000.0
D_FF = -(-8 * D_MODEL // (3 * 256)) * 256
D_PLE = 256
ALPHA = (2.0 * DEPTH) ** 0.25
BETA = (8.0 * DEPTH) ** -0.25
LN_EPS = 1e-5
IN_SIZES = (D_S5, D_RG, D_RG, D_ATTN, KV_DIM, KV_DIM, IDX_HEADS * IDX_DIM, IDX_DIM, IDX_HEADS)
N_IN = sum(IN_SIZES)

kernel_name = "hybrid_s5_rglru_dsa_deepnorm"


def _split_points():
    pts, acc = [], 0
    for s in IN_SIZES[:-1]:
        acc += s
        pts.append(acc)
    return pts


def layer_norm(x, g, b):
    xf = x.astype(jnp.float32)
    mu = jnp.mean(xf, axis=-1, keepdims=True)
    var = jnp.mean(jnp.square(xf - mu), axis=-1, keepdims=True)
    return ((xf - mu) * lax.rsqrt(var + LN_EPS) * g + b).astype(x.dtype)


def rope_tables(positions, dim):
    inv = ROPE_THETA ** (-jnp.arange(0, dim, 2, dtype=jnp.float32) / dim)
    ang = positions.astype(jnp.float32)[..., None] * inv
    return jnp.cos(ang)[:, :, None, :], jnp.sin(ang)[:, :, None, :]


def rope(x, cos, sin):
    cos = cos.astype(x.dtype)
    sin = sin.astype(x.dtype)
    x1, x2 = jnp.split(x, 2, axis=-1)
    return jnp.concatenate([x1 * cos - x2 * sin, x2 * cos + x1 * sin], axis=-1)


def _complex_linear_combine(e1, e2):
    a1r, a1i, b1r, b1i = e1
    a2r, a2i, b2r, b2i = e2
    ar = a2r * a1r - a2i * a1i
    ai = a2r * a1i + a2i * a1r
    br = a2r * b1r - a2i * b1i + b2r
    bi = a2r * b1i + a2i * b1r + b2i
    return ar, ai, br, bi


def _real_linear_combine(e1, e2):
    a1, b1 = e1
    a2, b2 = e2
    return a2 * a1, a2 * b1 + b2


def s5_mixer(u, lam_re, lam_im, log_step, b_re, b_im, c_re, c_im, d_skip, w_glu, b_glu):
    bsz, L, _ = u.shape
    uf = u.astype(jnp.float32)
    ug = uf.reshape(bsz, L, S5_GROUPS, S5_GROUP)
    lr = lam_re.astype(jnp.float32)
    li = lam_im.astype(jnp.float32)
    step = jnp.exp(log_step.astype(jnp.float32))[:, None]
    mag = jnp.exp(lr * step)
    ar = mag * jnp.cos(li * step)
    ai = mag * jnp.sin(li * step)
    den = lr * lr + li * li
    nr, ni = ar - 1.0, ai
    cr = (nr * lr + ni * li) / den
    ci = (ni * lr - nr * li) / den
    br = b_re.astype(jnp.float32)
    bi = b_im.astype(jnp.float32)
    bbr = cr[..., None] * br - ci[..., None] * bi
    bbi = cr[..., None] * bi + ci[..., None] * br
    bu_r = jnp.einsum('blgc,gpc->blgp', ug, bbr)
    bu_i = jnp.einsum('blgc,gpc->blgp', ug, bbi)
    a_r = jnp.broadcast_to(ar, bu_r.shape)
    a_i = jnp.broadcast_to(ai, bu_r.shape)
    _, _, xr, xi = lax.associative_scan(_complex_linear_combine, (a_r, a_i, bu_r, bu_i), axis=1)
    y = (jnp.einsum('blgp,gcp->blgc', xr, c_re.astype(jnp.float32))
         - jnp.einsum('blgp,gcp->blgc', xi, c_im.astype(jnp.float32)))
    y = y.reshape(bsz, L, D_S5) + d_skip.astype(jnp.float32) * uf
    y = jax.nn.gelu(y)
    y = y * jax.nn.sigmoid(y @ w_glu.astype(jnp.float32) + b_glu.astype(jnp.float32))
    return y.astype(u.dtype)


def rglru_mixer(xr, gate, conv_w, conv_b, wa, ba, wx, bx, lam):
    bsz, L, _ = xr.shape
    xpad = jnp.pad(xr, ((0, 0), (RG_CONV - 1, 0), (0, 0)))
    xc = conv_b
    for k in range(RG_CONV):
        xc = xc + conv_w[k] * xpad[:, k:k + L]
    xb = xc.reshape(bsz, L, RG_BLOCKS, RG_BLOCK)
    r = jax.nn.sigmoid(jnp.einsum('blhi,hij->blhj', xb, wa).reshape(bsz, L, D_RG) + ba)
    ig = jax.nn.sigmoid(jnp.einsum('blhi,hij->blhj', xb, wx).reshape(bsz, L, D_RG) + bx)
    log_a = -RG_C * r.astype(jnp.float32) * jax.nn.softplus(-lam.astype(jnp.float32))
    a = jnp.exp(log_a)
    mult = jnp.sqrt(-jnp.expm1(2.0 * log_a))
    bvals = mult * (ig * xc).astype(jnp.float32)
    _, h = lax.associative_scan(_real_linear_combine, (a, bvals), axis=1)
    return h.astype(xr.dtype) * jax.nn.gelu(gate)


def dsa_mixer(q, k, v, qi, ki, wi, positions):
    bsz, L, _ = q.shape
    q = q.reshape(bsz, L, N_HEADS, HEAD_DIM)
    k = k.reshape(bsz, L, N_KV_HEADS, HEAD_DIM)
    v = v.reshape(bsz, L, N_KV_HEADS, HEAD_DIM)
    qi = qi.reshape(bsz, L, IDX_HEADS, IDX_DIM)
    cos, sin = rope_tables(positions, HEAD_DIM)
    q = rope(q, cos, sin)
    k = rope(k, cos, sin)
    qi = rope(qi, cos, sin)
    ki = rope(ki[:, :, None, :], cos, sin)[:, :, 0]
    top = min(TOPK_MAX, L // 4)
    nblk = L // Q_BLOCK
    q_b = q.reshape(bsz, nblk, Q_BLOCK, N_KV_HEADS, Q_PER_KV, HEAD_DIM).transpose(1, 0, 2, 3, 4, 5)
    qi_b = qi.reshape(bsz, nblk, Q_BLOCK, IDX_HEADS, IDX_DIM).transpose(1, 0, 2, 3, 4)
    wi_b = wi.reshape(bsz, nblk, Q_BLOCK, IDX_HEADS).transpose(1, 0, 2, 3)
    key_pos = jnp.arange(L)
    idx_scale = (IDX_HEADS * IDX_DIM) ** -0.5
    att_scale = HEAD_DIM ** -0.5

    def block(args):
        blk, qb, qib, wib = args
        t = blk * Q_BLOCK + jnp.arange(Q_BLOCK)
        causal = key_pos[None, :] <= t[:, None]
        rel = jax.nn.relu(jnp.einsum('bthd,bsd->bths', qib, ki).astype(jnp.float32))
        score = jnp.einsum('bths,bth->bts', rel, wib.astype(jnp.float32)) * idx_scale
        score = jnp.where(causal[None], score, -jnp.inf)
        _, idx = lax.top_k(score, top)
        valid = idx <= t[None, :, None]
        k_sel = jax.vmap(lambda kk, ii: kk[ii])(k, idx)
        v_sel = jax.vmap(lambda vv, ii: vv[ii])(v, idx)
        logits = jnp.einsum('btgrd,btkgd->btgrk', qb, k_sel).astype(jnp.float32) * att_scale
        logits = jnp.where(valid[:, :, None, None, :], logits, -jnp.inf)
        probs = jax.nn.softmax(logits, axis=-1).astype(v.dtype)
        o = jnp.einsum('btgrk,btkgd->btgrd', probs, v_sel)
        return o.reshape(bsz, Q_BLOCK, D_ATTN)

    out = lax.map(block, (jnp.arange(nblk), q_b, qi_b, wi_b))
    return out.transpose(1, 0, 2, 3).reshape(bsz, L, D_ATTN)


def setup_inputs(seed: int = 0) -> dict:
    key = jax.random.key(seed)
    ks = iter(jax.random.split(key, 48))
    f32 = jnp.float32

    def nrm(shape, scale):
        return scale * jax.random.normal(next(ks), shape, f32)

    x = nrm((BATCH, SEQ, D_MODEL), 1.0)
    p = nrm((DEPTH, BATCH, SEQ, D_PLE), 1.0)
    offs = jax.random.randint(next(ks), (BATCH, 1), 0, 4096, dtype=jnp.int32)
    positions = (offs + jnp.arange(SEQ, dtype=jnp.int32)[None, :]).astype(jnp.int32)
    ln_emb_g = 1.0 + nrm((D_MODEL,), 0.02)
    ln_emb_b = nrm((D_MODEL,), 0.02)
    w_in = nrm((DEPTH, D_MODEL, N_IN), D_MODEL ** -0.5)
    s5_lam_re = -0.5 + nrm((DEPTH, S5_GROUPS, S5_STATE), 0.01)
    s5_lam_im = math.pi * jnp.arange(S5_STATE, dtype=f32) + nrm((DEPTH, S5_GROUPS, S5_STATE), 0.01)
    s5_log_step = jax.random.uniform(next(ks), (DEPTH, S5_GROUPS), f32, math.log(1e-3), math.log(1e-1))
    s5_b_re = nrm((DEPTH, S5_GROUPS, S5_STATE, S5_GROUP), (2 * S5_GROUP) ** -0.5)
    s5_b_im = nrm((DEPTH, S5_GROUPS, S5_STATE, S5_GROUP), (2 * S5_GROUP) ** -0.5)
    s5_c_re = nrm((DEPTH, S5_GROUPS, S5_GROUP, S5_STATE), S5_STATE ** -0.5)
    s5_c_im = nrm((DEPTH, S5_GROUPS, S5_GROUP, S5_STATE), S5_STATE ** -0.5)
    s5_d = nrm((DEPTH, D_S5), 1.0)
    s5_w_glu = nrm((DEPTH, D_S5, D_S5), D_S5 ** -0.5)
    s5_b_glu = nrm((DEPTH, D_S5), 0.02)
    rg_conv_w = nrm((DEPTH, RG_CONV, D_RG), RG_CONV ** -0.5)
    rg_conv_b = nrm((DEPTH, D_RG), 0.02)
    rg_wa = nrm((DEPTH, RG_BLOCKS, RG_BLOCK, RG_BLOCK), RG_BLOCK ** -0.5)
    rg_ba = nrm((DEPTH, D_RG), 0.02)
    rg_wx = nrm((DEPTH, RG_BLOCKS, RG_BLOCK, RG_BLOCK), RG_BLOCK ** -0.5)
    rg_bx = nrm((DEPTH, D_RG), 0.02)
    a_c = jax.random.uniform(next(ks), (DEPTH, D_RG), f32, 0.9, 0.999)
    a0 = a_c ** (1.0 / RG_C)
    rg_lam = jnp.log(a0) - jnp.log1p(-a0)
    w_out = nrm((DEPTH, D_MIX, D_MODEL), BETA * D_MIX ** -0.5)
    ln1_g = 1.0 + nrm((DEPTH, D_MODEL), 0.02)
    ln1_b = nrm((DEPTH, D_MODEL), 0.02)
    ffn_w_up = nrm((DEPTH, D_MODEL, 2 * D_FF), D_MODEL ** -0.5)
    ffn_w_down = nrm((DEPTH, D_FF, D_MODEL), BETA * D_FF ** -0.5)
    ple_w_gate = nrm((DEPTH, D_MODEL, D_MODEL), D_MODEL ** -0.5)
    ple_w_proj = nrm((DEPTH, D_PLE, D_MODEL), BETA * D_PLE ** -0.5)
    ln2_g = 1.0 + nrm((DEPTH, D_MODEL), 0.02)
    ln2_b = nrm((DEPTH, D_MODEL), 0.02)
    return {
        "x": x, "p": p, "positions": positions,
        "ln_emb_g": ln_emb_g, "ln_emb_b": ln_emb_b, "w_in": w_in,
        "s5_lam_re": s5_lam_re, "s5_lam_im": s5_lam_im, "s5_log_step": s5_log_step,
        "s5_b_re": s5_b_re, "s5_b_im": s5_b_im, "s5_c_re": s5_c_re, "s5_c_im": s5_c_im,
        "s5_d": s5_d, "s5_w_glu": s5_w_glu, "s5_b_glu": s5_b_glu,
        "rg_conv_w": rg_conv_w, "rg_conv_b": rg_conv_b, "rg_wa": rg_wa, "rg_ba": rg_ba,
        "rg_wx": rg_wx, "rg_bx": rg_bx, "rg_lam": rg_lam,
        "w_out": w_out, "ln1_g": ln1_g, "ln1_b": ln1_b,
        "ffn_w_up": ffn_w_up, "ffn_w_down": ffn_w_down,
        "ple_w_gate": ple_w_gate, "ple_w_proj": ple_w_proj,
        "ln2_g": ln2_g, "ln2_b": ln2_b,
    }


def reference(x, p, positions, ln_emb_g, ln_emb_b, w_in,
              s5_lam_re, s5_lam_im, s5_log_step, s5_b_re, s5_b_im, s5_c_re, s5_c_im,
              s5_d, s5_w_glu, s5_b_glu,
              rg_conv_w, rg_conv_b, rg_wa, rg_ba, rg_wx, rg_bx, rg_lam,
              w_out, ln1_g, ln1_b, ffn_w_up, ffn_w_down, ple_w_gate, ple_w_proj,
              ln2_g, ln2_b):
    pts = _split_points()
    h = layer_norm(x, ln_emb_g, ln_emb_b)
    for i in range(DEPTH):
        proj = h @ w_in[i]
        u_s5, x_rg, g_rg, q, k, v, qi, ki, wi = jnp.split(proj, pts, axis=-1)
        y_s5 = s5_mixer(u_s5, s5_lam_re[i], s5_lam_im[i], s5_log_step[i], s5_b_re[i], s5_b_im[i],
                        s5_c_re[i], s5_c_im[i], s5_d[i], s5_w_glu[i], s5_b_glu[i])
        y_rg = rglru_mixer(x_rg, g_rg, rg_conv_w[i], rg_conv_b[i], rg_wa[i], rg_ba[i],
                           rg_wx[i], rg_bx[i], rg_lam[i])
        y_at = dsa_mixer(q, k, v, qi, ki, wi, positions)
        mix = jnp.concatenate([y_s5, y_rg, y_at], axis=-1) @ w_out[i]
        h = layer_norm(ALPHA * h + mix, ln1_g[i], ln1_b[i])
        gate_up = h @ ffn_w_up[i]
        g_ff, u_ff = jnp.split(gate_up, 2, axis=-1)
        ffn = (jax.nn.silu(g_ff) * u_ff) @ ffn_w_down[i]
        ple = jax.nn.sigmoid(h @ ple_w_gate[i]) * (p[i] @ ple_w_proj[i])
        h = layer_norm(ALPHA * h + ffn + ple, ln2_g[i], ln2_b[i])
    return h
```

```python
import functools
import math

import jax
import jax.numpy as jnp
from jax import lax
from jax.experimental import pallas as pl
from jax.experimental.pallas import tpu as pltpu

F32 = jnp.float32
BF16 = jnp.bfloat16
I32 = jnp.int32

D_MODEL = 1024
BATCH = 8
SEQ = 2048
DEPTH = 2
D_S5 = 256
S5_GROUP = 16
S5_GROUPS = 16
S5_STATE = 64
N_S5_STATE = S5_GROUPS * S5_STATE
D_RG = 256
RG_BLOCKS = 8
RG_BLOCK = 32
RG_CONV = 4
RG_C = 8.0
N_HEADS = 8
N_KV_HEADS = 2
HEAD_DIM = 64
D_ATTN = 512
KV_DIM = 128
IDX_HEADS = 8
IDX_DIM = 64
TOPK_MAX = 256
D_MIX = 1024
ROPE_THETA = 10000.0
D_FF = 2816
D_PLE = 256
ALPHA = (2.0 * DEPTH) ** 0.25
LN_EPS = 1e-5
N_IN = 2120
N_IN_PAD = 2176
ATT_SCALE = HEAD_DIM ** -0.5
IDX_SCALE = (IDX_HEADS * IDX_DIM) ** -0.5

LANES = 128
SUBLANES = 8
INT_MIN = -2 ** 31
NEG_INF_KEY = (0xFF800000 ^ 0x7FFFFFFF) - 2 ** 32

VMEM_LIMIT = 56 * 1024 * 1024


def _dot(a, b):
    return jnp.dot(a, b, preferred_element_type=F32)


def _dot_nt(a, b):
    return lax.dot_general(a, b, (((1,), (1,)), ((), ())), preferred_element_type=F32)


def _ln(x, g, b):
    mu = jnp.mean(x, axis=-1, keepdims=True)
    xc = x - mu
    var = jnp.mean(xc * xc, axis=-1, keepdims=True)
    return xc * lax.rsqrt(var + LN_EPS) * g + b


def _gelu(x):
    c = math.sqrt(2.0 / math.pi)
    return 0.5 * x * (1.0 + jnp.tanh(c * (x + 0.044715 * (x * x * x))))


def _sigmoid(x):
    return 1.0 / (1.0 + jnp.exp(-x))


def _full_spec(shape):
    n = len(shape)
    return pl.BlockSpec(shape, lambda *_: (0,) * n)


def _rope_table_kernel(pos_ref, inv_ref, sgn_ref, cos_ref, sin_ref):
    ang = pos_ref[...].astype(F32) * inv_ref[...]
    cos_ref[...] = jnp.cos(ang)
    sin_ref[...] = jnp.sin(ang) * sgn_ref[...]


def _rope_tables(positions, seq):
    t = BATCH * seq
    tm = min(t, 2048)
    inv = ROPE_THETA ** (-jnp.arange(0, HEAD_DIM, 2, dtype=F32) / HEAD_DIM)
    inv128 = jnp.tile(inv, 4)[None, :]
    sgn = jnp.where((jnp.arange(LANES) % HEAD_DIM) < HEAD_DIM // 2, -1.0, 1.0).astype(F32)[None, :]
    pos = positions.reshape(t, 1)
    cos, sin = pl.pallas_call(
        _rope_table_kernel,
        out_shape=(jax.ShapeDtypeStruct((t, LANES), F32),) * 2,
        grid=(t // tm,),
        in_specs=[pl.BlockSpec((tm, 1), lambda i: (i, 0)), _full_spec((1, LANES)), _full_spec((1, LANES))],
        out_specs=(pl.BlockSpec((tm, LANES), lambda i: (i, 0)),) * 2,
        compiler_params=pltpu.CompilerParams(dimension_semantics=("parallel",)),
        name="rope_tables",
    )(pos, inv128, sgn)
    return cos.reshape(BATCH, seq, LANES), sin.reshape(BATCH, seq, LANES)


def _inproj_kernel(apply_ln, tt, h_ref, w_ref, cos_ref, sin_ref, g_ref, b_ref, *outs):
    if apply_ln:
        hn_ref, so_ref, q_ref, kz_ref, v_ref, qi_ref, kiz_ref, wi_ref = outs
    else:
        so_ref, q_ref, kz_ref, v_ref, qi_ref, kiz_ref, wi_ref = outs
    rows = BATCH * tt
    h = h_ref[...].reshape(rows, D_MODEL)
    if apply_ln:
        h = _ln(h, g_ref[...], b_ref[...])
        hn_ref[...] = h.reshape(BATCH, tt, D_MODEL)
    hb = h.astype(BF16)
    cos = cos_ref[...].reshape(rows, LANES)
    sin = sin_ref[...].reshape(rows, LANES)
    lane = lax.broadcasted_iota(I32, (rows, LANES), 1)
    first_half = (lane & (HEAD_DIM // 2)) == 0
    lo64 = lane < HEAD_DIM

    def rope(x, cs, sn):
        partner = jnp.where(first_half, pltpu.roll(x, LANES - 32, 1), pltpu.roll(x, 32, 1))
        return x * cs + partner * sn

    def rope_wide(x, n_chunks):
        return jnp.concatenate(
            [rope(x[:, c * LANES:(c + 1) * LANES], cos, sin) for c in range(n_chunks)], axis=1)

    ps = _dot(hb, w_ref[:, 0:768])
    for j in range(6):
        for b in range(BATCH):
            so_ref[j, pl.ds(b, tt, stride=BATCH), :] = ps[b * tt:(b + 1) * tt, j * LANES:(j + 1) * LANES]

    pq = _dot(hb, w_ref[:, 768:1280])
    q_ref[...] = (rope_wide(pq, 4) * ATT_SCALE).astype(BF16).reshape(BATCH, tt, D_ATTN)

    pkv = _dot(hb, w_ref[:, 1280:1536])
    kr = rope(pkv[:, 0:LANES], cos, sin)
    ksw = pltpu.roll(kr, HEAD_DIM, 1)
    zero = jnp.zeros_like(kr)
    kz = jnp.concatenate([jnp.where(lo64, kr, zero), jnp.where(lo64, zero, ksw),
                          jnp.where(lo64, ksw, zero), jnp.where(lo64, zero, kr)], axis=1)
    kz_ref[...] = kz.astype(BF16).reshape(BATCH, tt, 4 * LANES)
    v_ref[...] = pkv[:, LANES:2 * LANES].astype(BF16).reshape(BATCH, tt, KV_DIM)

    pqi = _dot(hb, w_ref[:, 1536:2048])
    qi_ref[...] = rope_wide(pqi, 4).astype(BF16).reshape(BATCH, tt, IDX_HEADS * IDX_DIM)

    pk = _dot(hb, w_ref[:, 2048:N_IN_PAD])
    kir = rope(pk, jnp.where(lo64, cos, 1.0), jnp.where(lo64, sin, 0.0))
    kie = jnp.where(lo64, kir, zero)
    kiz = jnp.concatenate([kie, pltpu.roll(kie, HEAD_DIM, 1)], axis=1)
    kiz_ref[...] = kiz.astype(BF16).reshape(BATCH, tt, 2 * LANES)
    wi_ref[...] = pk.reshape(BATCH, tt, LANES)


def _inproj(h3, w_in_p, cos, sin, ln_g, ln_b, apply_ln, seq):
    tt = min(seq, 128)
    nt = seq // tt
    blk = lambda c: pl.BlockSpec((BATCH, tt, c), lambda i: (0, i, 0))
    out_shape = [
        jax.ShapeDtypeStruct((6, seq * BATCH, LANES), F32),
        jax.ShapeDtypeStruct((BATCH, seq, D_ATTN), BF16),
        jax.ShapeDtypeStruct((BATCH, seq, 4 * LANES), BF16),
        jax.ShapeDtypeStruct((BATCH, seq, KV_DIM), BF16),
        jax.ShapeDtypeStruct((BATCH, seq, IDX_HEADS * IDX_DIM), BF16),
        jax.ShapeDtypeStruct((BATCH, seq, 2 * LANES), BF16),
        jax.ShapeDtypeStruct((BATCH, seq, LANES), F32),
    ]
    out_specs = [pl.BlockSpec((6, tt * BATCH, LANES), lambda i: (0, i, 0)),
                 blk(D_ATTN), blk(4 * LANES), blk(KV_DIM), blk(512), blk(2 * LANES), blk(LANES)]
    if apply_ln:
        out_shape = [jax.ShapeDtypeStruct((BATCH, seq, D_MODEL), F32)] + out_shape
        out_specs = [blk(D_MODEL)] + out_specs
    return pl.pallas_call(
        functools.partial(_inproj_kernel, apply_ln, tt),
        out_shape=tuple(out_shape),
        grid=(nt,),
        in_specs=[blk(D_MODEL), _full_spec((D_MODEL, N_IN_PAD)), blk(LANES), blk(LANES),
                  _full_spec((1, D_MODEL)), _full_spec((1, D_MODEL))],
        out_specs=tuple(out_specs),
        compiler_params=pltpu.CompilerParams(dimension_semantics=("parallel",),
                                             vmem_limit_bytes=VMEM_LIMIT),
        name="inproj_ln" if apply_ln else "inproj",
    )(h3, w_in_p, cos, sin, ln_g, ln_b)


def _s5_param_kernel(lr_ref, li_ref, ls_ref, br_ref, bi_ref, a_ref, bd_ref):
    lr = lr_ref[...]
    li = li_ref[...]
    step = jnp.exp(ls_ref[...])
    mag = jnp.exp(lr * step)
    ar = mag * jnp.cos(li * step)
    ai = mag * jnp.sin(li * step)
    den = lr * lr + li * li
    nr, ni = ar - 1.0, ai
    cr = (nr * lr + ni * li) / den
    ci = (ni * lr - nr * li) / den
    br = br_ref[...]
    bi = bi_ref[...]
    bbr = cr * br - ci * bi
    bbi = cr * bi + ci * br
    row = lax.broadcasted_iota(I32, (D_S5, N_S5_STATE), 0)
    col = lax.broadcasted_iota(I32, (D_S5, N_S5_STATE), 1)
    blk = (row // S5_GROUP) == (col // S5_STATE)
    zero = jnp.zeros((D_S5, N_S5_STATE), F32)
    bd_ref[:, 0:N_S5_STATE] = jnp.where(blk, jnp.concatenate([bbr] * S5_GROUPS, axis=0), zero).astype(BF16)
    bd_ref[:, N_S5_STATE:] = jnp.where(blk, jnp.concatenate([bbi] * S5_GROUPS, axis=0), zero).astype(BF16)
    a_ref[:, 0:N_S5_STATE] = jnp.broadcast_to(ar, (SUBLANES, N_S5_STATE))
    a_ref[:, N_S5_STATE:] = jnp.broadcast_to(ai, (SUBLANES, N_S5_STATE))


def _s5_params(lam_re, lam_im, log_step, b_re, b_im):
    lr = lam_re.reshape(1, N_S5_STATE)
    li = lam_im.reshape(1, N_S5_STATE)
    ls = jnp.repeat(log_step, S5_STATE).reshape(1, N_S5_STATE)
    br = b_re.transpose(2, 0, 1).reshape(S5_GROUP, N_S5_STATE)
    bi = b_im.transpose(2, 0, 1).reshape(S5_GROUP, N_S5_STATE)
    return pl.pallas_call(
        _s5_param_kernel,
        out_shape=(jax.ShapeDtypeStruct((SUBLANES, 2 * N_S5_STATE), F32),
                   jax.ShapeDtypeStruct((D_S5, 2 * N_S5_STATE), BF16)),
        name="s5_params",
    )(lr, li, ls, br, bi)


def _scan_kernel(tt, si_ref, a_ref, bd_ref, cd_ref, dsk_ref, wglu_ref, bglu_ref,
                 cw_ref, cb_ref, wa_ref, ba_ref, wx_ref, bx_ref, lam_ref,
                 o_ref, x_ref, hs_ref, halo_ref, ab_ref, hrg_ref, y_ref):
    rows = BATCH * tt
    i = pl.program_id(0)

    @pl.when(i == 0)
    def _init():
        hs_ref[...] = jnp.zeros_like(hs_ref)
        halo_ref[...] = jnp.zeros_like(halo_ref)
        hrg_ref[...] = jnp.zeros_like(hrg_ref)

    u = jnp.concatenate([si_ref[0], si_ref[1]], axis=1)
    x_ref[...] = _dot(u.astype(BF16), bd_ref[...])
    half = N_S5_STATE // 2
    for hh in range(2):
        lo = hh * half
        ar = a_ref[:, lo:lo + half]
        ai = a_ref[:, N_S5_STATE + lo:N_S5_STATE + lo + half]

        def step(t, carry, lo=lo, ar=ar, ai=ai):
            hr, hi = carry
            r0 = pl.multiple_of(t * BATCH, BATCH)
            br = x_ref[pl.ds(r0, BATCH), lo:lo + half]
            bi = x_ref[pl.ds(r0, BATCH), N_S5_STATE + lo:N_S5_STATE + lo + half]
            nr = ar * hr - ai * hi + br
            ni = ar * hi + ai * hr + bi
            x_ref[pl.ds(r0, BATCH), lo:lo + half] = nr
            x_ref[pl.ds(r0, BATCH), N_S5_STATE + lo:N_S5_STATE + lo + half] = ni
            return nr, ni

        hr, hi = lax.fori_loop(
            0, tt, step,
            (hs_ref[:, lo:lo + half], hs_ref[:, N_S5_STATE + lo:N_S5_STATE + lo + half]),
            unroll=4)
        hs_ref[:, lo:lo + half] = hr
        hs_ref[:, N_S5_STATE + lo:N_S5_STATE + lo + half] = hi

    y = _dot(x_ref[...].astype(BF16), cd_ref[...]) + dsk_ref[...] * u
    y = _gelu(y)
    y = y * _sigmoid(_dot(y.astype(BF16), wglu_ref[...]) + bglu_ref[...])
    y_ref[0] = y[:, 0:LANES]
    y_ref[1] = y[:, LANES:2 * LANES]

    xr = jnp.concatenate([si_ref[2], si_ref[3]], axis=1)
    gate = jnp.concatenate([si_ref[4], si_ref[5]], axis=1)
    hal = (RG_CONV - 1) * BATCH
    xext = jnp.concatenate([halo_ref[...], xr], axis=0)
    halo_ref[...] = xr[rows - hal:rows, :]
    xc = cb_ref[...]
    for k in range(RG_CONV):
        xc = xc + cw_ref[k:k + 1, :] * xext[k * BATCH:k * BATCH + rows, :]
    xcb = xc.astype(BF16)
    r = _sigmoid(_dot(xcb, wa_ref[...]) + ba_ref[...])
    ig = _sigmoid(_dot(xcb, wx_ref[...]) + bx_ref[...])
    nl = -lam_ref[...]
    softplus = jnp.maximum(nl, 0.0) + jnp.log(1.0 + jnp.exp(-jnp.abs(nl)))
    log_a = -RG_C * r * softplus
    a = jnp.exp(log_a)
    mult = jnp.sqrt(1.0 - a * a)
    ab_ref[0] = a
    ab_ref[1] = mult * (ig * xc)

    def rg_step(t, h):
        r0 = pl.multiple_of(t * BATCH, BATCH)
        hn = ab_ref[0, pl.ds(r0, BATCH), :] * h + ab_ref[1, pl.ds(r0, BATCH), :]
        ab_ref[1, pl.ds(r0, BATCH), :] = hn
        return hn

    hrg_ref[...] = lax.fori_loop(0, tt, rg_step, hrg_ref[...], unroll=8)
    yr = ab_ref[1] * _gelu(gate)
    y_ref[2] = yr[:, 0:LANES]
    y_ref[3] = yr[:, LANES:2 * LANES]

    for b in range(BATCH):
        o_ref[b] = jnp.concatenate(
            [y_ref[j, pl.ds(b, tt, stride=BATCH), :] for j in range(4)], axis=1).astype(BF16)


def _scans(scan_in, a8, bd, cd, dsk, wglu, bglu, cw, cb, wa, ba, wx, bx, lam, seq):
    tt = min(seq, 64)
    nt = seq // tt
    rows = tt * BATCH
    params = [a8, bd, cd, dsk, wglu, bglu, cw, cb, wa, ba, wx, bx, lam]
    return pl.pallas_call(
        functools.partial(_scan_kernel, tt),
        out_shape=jax.ShapeDtypeStruct((BATCH, seq, 2 * D_S5), BF16),
        grid=(nt,),
        in_specs=[pl.BlockSpec((6, rows, LANES), lambda i: (0, i, 0))] + [_full_spec(p.shape) for p in params],
        out_specs=pl.BlockSpec((BATCH, tt, 2 * D_S5), lambda i: (0, i, 0)),
        scratch_shapes=[
            pltpu.VMEM((rows, 2 * N_S5_STATE), F32),
            pltpu.VMEM((BATCH, 2 * N_S5_STATE), F32),
            pltpu.VMEM(((RG_CONV - 1) * BATCH, D_RG), F32),
            pltpu.VMEM((2, rows, D_RG), F32),
            pltpu.VMEM((BATCH, D_RG), F32),
            pltpu.VMEM((4, rows, LANES), F32),
        ],
        compiler_params=pltpu.CompilerParams(dimension_semantics=("arbitrary",),
                                             vmem_limit_bytes=VMEM_LIMIT),
        name="scans",
    )(scan_in, *params)


def _attn_kernel(seq, top, ck, q_ref, kz_ref, v_ref, qi_ref, kiz_ref, wi_ref, o_ref,
                 vt_ref, key_ref, idx_ref, bias_ref, l_ref, p_ref):
    j = pl.program_id(1)
    nc = seq // ck
    qb = LANES

    @pl.when(j == 0)
    def _():
        vt_ref[...] = v_ref[...].astype(F32).T.astype(BF16)

    t_idx = j * qb + lax.broadcasted_iota(I32, (ck, qb), 1)
    s_iota = lax.broadcasted_iota(I32, (ck, qb), 0)

    wts = wi_ref[...].T[HEAD_DIM:HEAD_DIM + IDX_HEADS, :] * IDX_SCALE
    qi = qi_ref[...]
    qs = jnp.concatenate([qi[:, c * LANES:(c + 1) * LANES] for c in range(4)], axis=0)

    def idx_body(c, carry):
        r0 = pl.multiple_of(c * ck, ck)
        se = _dot_nt(kiz_ref[pl.ds(r0, ck), 0:LANES], qs)
        so = _dot_nt(kiz_ref[pl.ds(r0, ck), LANES:2 * LANES], qs)
        acc = jnp.zeros((ck, qb), F32)
        for c4 in range(4):
            acc = acc + wts[2 * c4:2 * c4 + 1, :] * jnp.maximum(se[:, c4 * qb:(c4 + 1) * qb], 0.0)
            acc = acc + wts[2 * c4 + 1:2 * c4 + 2, :] * jnp.maximum(so[:, c4 * qb:(c4 + 1) * qb], 0.0)
        acc = jnp.where(r0 + s_iota <= t_idx, acc, -jnp.inf)
        bits = pltpu.bitcast(acc, I32)
        key_ref[pl.ds(r0, ck), :] = jnp.where(bits < 0, bits ^ 0x7FFFFFFF, bits)
        return carry

    lax.fori_loop(0, nc, idx_body, 0)

    def count(ref, cand, op):
        def body(c, acc):
            r0 = pl.multiple_of(c * ck, ck)
            k = ref[pl.ds(r0, ck), :]
            m = op(k, cand)
            return acc + jnp.sum(jnp.where(m, 1, 0).astype(I32).reshape(ck // SUBLANES, SUBLANES, qb), axis=0)
        acc = lax.fori_loop(0, nc, body, jnp.zeros((SUBLANES, qb), I32))
        return jnp.sum(acc, axis=0, keepdims=True)

    ge = lambda a, b: a >= b
    gt = lambda a, b: a > b
    lt = lambda a, b: a < b

    def bit_body(i, tu):
        cand = tu | jnp.left_shift(jnp.int32(1), 31 - i)
        cnt = count(key_ref, cand ^ INT_MIN, ge)
        return jnp.where(cnt >= top, cand, tu)

    tu = lax.fori_loop(0, 32, bit_body, jnp.zeros((1, qb), I32))
    thr = tu ^ INT_MIN
    n_ge = count(key_ref, thr, ge)
    n_gt = count(key_ref, thr, gt)
    need = top - n_gt
    tie = jnp.logical_and(n_ge > top, thr != NEG_INF_KEY)
    any_tie = jnp.max(jnp.where(tie, 1, 0).astype(I32)) > 0

    def bias_default(c, carry):
        r0 = pl.multiple_of(c * ck, ck)
        k = key_ref[pl.ds(r0, ck), :]
        sel = jnp.logical_and(k >= thr, r0 + s_iota <= t_idx)
        bias_ref[pl.ds(r0, ck), :] = jnp.where(sel, 0.0, -jnp.inf).astype(F32)
        return carry

    lax.fori_loop(0, nc, bias_default, 0)

    @pl.when(any_tie)
    def _tie():
        big = jnp.int32(2 * seq)

        def fill(c, carry):
            r0 = pl.multiple_of(c * ck, ck)
            k = key_ref[pl.ds(r0, ck), :]
            idx_ref[pl.ds(r0, ck), :] = jnp.where(k == thr, r0 + s_iota, big)
            return carry

        lax.fori_loop(0, nc, fill, 0)
        nbits = seq.bit_length() - 1

        def bit2(i, m):
            cand = m | jnp.left_shift(jnp.int32(1), nbits - 1 - i)
            cnt = count(idx_ref, cand, lt)
            return jnp.where(cnt < need, cand, m)

        m = lax.fori_loop(0, nbits, bit2, jnp.zeros((1, qb), I32))

        def bias_tie(c, carry):
            r0 = pl.multiple_of(c * ck, ck)
            k = key_ref[pl.ds(r0, ck), :]
            sel = jnp.logical_or(k > thr, idx_ref[pl.ds(r0, ck), :] <= m)
            sel = jnp.logical_and(sel, r0 + s_iota <= t_idx)
            bias_ref[pl.ds(r0, ck), :] = jnp.where(sel, 0.0, -jnp.inf).astype(F32)
            return carry

        lax.fori_loop(0, nc, bias_tie, 0)

    q = q_ref[...]
    o_parts = []
    for g in range(N_KV_HEADS):
        qe = jnp.concatenate([q[:, (2 * g) * LANES:(2 * g + 1) * LANES],
                              q[:, (2 * g + 1) * LANES:(2 * g + 2) * LANES]], axis=0)

        def lbody(c, m8, g=g, qe=qe):
            r0 = pl.multiple_of(c * ck, ck)
            le = _dot_nt(kz_ref[pl.ds(r0, ck), (2 * g) * LANES:(2 * g + 1) * LANES], qe)
            lo = _dot_nt(kz_ref[pl.ds(r0, ck), (2 * g + 1) * LANES:(2 * g + 2) * LANES], qe)
            b = bias_ref[pl.ds(r0, ck), :]
            l = jnp.concatenate([le, lo], axis=1) + jnp.concatenate([b] * 4, axis=1)
            l_ref[pl.ds(r0, ck), :] = l
            return jnp.maximum(m8, jnp.max(l.reshape(ck // SUBLANES, SUBLANES, 4 * qb), axis=0))

        m8 = lax.fori_loop(0, nc, lbody, jnp.full((SUBLANES, 4 * qb), -jnp.inf, F32))
        mx = jnp.max(m8, axis=0, keepdims=True)

        def pbody(c, s8, mx=mx):
            r0 = pl.multiple_of(c * ck, ck)
            p = jnp.exp(l_ref[pl.ds(r0, ck), :] - mx)
            p_ref[pl.ds(r0, ck), :] = p.astype(BF16)
            return s8 + jnp.sum(p.reshape(ck // SUBLANES, SUBLANES, 4 * qb), axis=0)

        s8 = lax.fori_loop(0, nc, pbody, jnp.zeros((SUBLANES, 4 * qb), F32))
        denom = jnp.sum(s8, axis=0, keepdims=True)
        ot = _dot(vt_ref[g * HEAD_DIM:(g + 1) * HEAD_DIM, :], p_ref[...])
        o_parts.append(ot * (1.0 / denom))
    ot = jnp.concatenate(o_parts, axis=0)
    y = jnp.concatenate([ot[:, i * qb:(i + 1) * qb].T for i in range(4)], axis=1)
    o_ref[...] = y.astype(BF16)


def _attention(q, kz, v, qi, kiz, wi, seq, top):
    ck = min(seq, 256)
    qb = LANES
    sq = pl.Squeezed()
    per_q = lambda c: pl.BlockSpec((sq, qb, c), lambda b, j: (b, j, 0))
    per_b = lambda c: pl.BlockSpec((sq, seq, c), lambda b, j: (b, 0, 0))
    return pl.pallas_call(
        functools.partial(_attn_kernel, seq, top, ck),
        out_shape=jax.ShapeDtypeStruct((BATCH, seq, D_ATTN), BF16),
        grid=(BATCH, seq // qb),
        in_specs=[per_q(D_ATTN), per_b(4 * LANES), per_b(KV_DIM), per_q(512), per_b(2 * LANES), per_q(LANES)],
        out_specs=per_q(D_ATTN),
        scratch_shapes=[
            pltpu.VMEM((KV_DIM, seq), BF16),
            pltpu.VMEM((seq, qb), I32),
            pltpu.VMEM((seq, qb), I32),
            pltpu.VMEM((seq, qb), F32),
            pltpu.VMEM((seq, 4 * qb), F32),
            pltpu.VMEM((seq, 4 * qb), BF16),
        ],
        compiler_params=pltpu.CompilerParams(dimension_semantics=("parallel", "arbitrary"),
                                             vmem_limit_bytes=VMEM_LIMIT),
        name="dsa_attention",
    )(q, kz, v, qi, kiz, wi)


def _outproj_kernel(ys_ref, ya_ref, h_ref, w_ref, g_ref, b_ref, o_ref):
    mix = _dot(ys_ref[...], w_ref[0:2 * D_S5, :]) + _dot(ya_ref[...], w_ref[2 * D_S5:, :])
    o_ref[...] = _ln(ALPHA * h_ref[...] + mix, g_ref[...], b_ref[...])


def _outproj(y_scan, y_at, h, w_out_p, g, b):
    t = h.shape[0]
    tm = min(t, 512)
    row = lambda c: pl.BlockSpec((tm, c), lambda i: (i, 0))
    return pl.pallas_call(
        _outproj_kernel,
        out_shape=jax.ShapeDtypeStruct((t, D_MODEL), F32),
        grid=(t // tm,),
        in_specs=[row(2 * D_S5), row(D_ATTN), row(D_MODEL), _full_spec((D_MIX, D_MODEL)),
                  _full_spec((1, D_MODEL)), _full_spec((1, D_MODEL))],
        out_specs=row(D_MODEL),
        compiler_params=pltpu.CompilerParams(dimension_semantics=("parallel",),
                                             vmem_limit_bytes=VMEM_LIMIT),
        name="outproj_ln",
    )(y_scan, y_at, h, w_out_p, g, b)


def _ffn_kernel(h_ref, p_ref, wg_ref, wu_ref, wd_ref, wpg_ref, wpp_ref, g_ref, b_ref, o_ref):
    h = h_ref[...]
    hb = h.astype(BF16)
    gate = _dot(hb, wg_ref[...])
    up = _dot(hb, wu_ref[...])
    act = (gate * _sigmoid(gate) * up).astype(BF16)
    ffn = _dot(act, wd_ref[...])
    ple = _sigmoid(_dot(hb, wpg_ref[...])) * _dot(p_ref[...].astype(BF16), wpp_ref[...])
    o_ref[...] = _ln(ALPHA * h + ffn + ple, g_ref[...], b_ref[...])


def _ffn(h, p, wg, wu, wd, wpg, wpp, g, b):
    t = h.shape[0]
    tm = min(t, 512)
    row = lambda c: pl.BlockSpec((tm, c), lambda i: (i, 0))
    const = lambda shape: pl.BlockSpec(shape, lambda i: (0, 0), pipeline_mode=pl.Buffered(1))
    return pl.pallas_call(
        _ffn_kernel,
        out_shape=jax.ShapeDtypeStruct((t, D_MODEL), F32),
        grid=(t // tm,),
        in_specs=[row(D_MODEL), row(D_PLE), const((D_MODEL, D_FF)), const((D_MODEL, D_FF)),
                  const((D_FF, D_MODEL)), const((D_MODEL, D_MODEL)), const((D_PLE, D_MODEL)),
                  _full_spec((1, D_MODEL)), _full_spec((1, D_MODEL))],
        out_specs=row(D_MODEL),
        compiler_params=pltpu.CompilerParams(dimension_semantics=("parallel",),
                                             vmem_limit_bytes=VMEM_LIMIT),
        name="ffn_ple_ln",
    )(h, p, wg, wu, wd, wpg, wpp, g, b)


def _block_diag(w):
    hh, n, m = w.shape
    eye = jnp.eye(hh, dtype=w.dtype)
    return jnp.einsum('hij,hk->hikj', w, eye).reshape(hh * n, hh * m)


def _s5_out_matrix(c_re, c_im):
    cre = _block_diag(c_re.transpose(0, 2, 1))
    cim = _block_diag(c_im.transpose(0, 2, 1))
    return jnp.concatenate([cre, -cim], axis=0).astype(BF16)


def _attn_out_rows(w_at):
    w = w_at.reshape(N_KV_HEADS, 4, HEAD_DIM, D_MODEL)
    w = w[:, jnp.array([0, 2, 1, 3])]
    return w.transpose(1, 0, 2, 3).reshape(D_ATTN, D_MODEL)


def _forward(seq, top, x, p, positions, ln_emb_g, ln_emb_b, w_in,
             s5_lam_re, s5_lam_im, s5_log_step, s5_b_re, s5_b_im, s5_c_re, s5_c_im,
             s5_d, s5_w_glu, s5_b_glu,
             rg_conv_w, rg_conv_b, rg_wa, rg_ba, rg_wx, rg_bx, rg_lam,
             w_out, ln1_g, ln1_b, ffn_w_up, ffn_w_down, ple_w_gate, ple_w_proj,
             ln2_g, ln2_b):
    t = BATCH * seq
    cos, sin = _rope_tables(positions, seq)
    row = lambda a: a.reshape(1, -1)
    h3 = x
    for i in range(DEPTH):
        w_in_p = jnp.pad(w_in[i], ((0, 0), (0, N_IN_PAD - N_IN))).astype(BF16)
        outs = _inproj(h3, w_in_p, cos, sin, row(ln_emb_g), row(ln_emb_b), i == 0, seq)
        if i == 0:
            h3, outs = outs[0], outs[1:]
        scan_in, q, kz, v, qi, kiz, wi = outs
        a8, bd = _s5_params(s5_lam_re[i], s5_lam_im[i], s5_log_step[i], s5_b_re[i], s5_b_im[i])
        y_scan = _scans(scan_in, a8, bd, _s5_out_matrix(s5_c_re[i], s5_c_im[i]), row(s5_d[i]),
                        s5_w_glu[i].astype(BF16), row(s5_b_glu[i]),
                        rg_conv_w[i], row(rg_conv_b[i]),
                        _block_diag(rg_wa[i]).astype(BF16), row(rg_ba[i]),
                        _block_diag(rg_wx[i]).astype(BF16), row(rg_bx[i]), row(rg_lam[i]), seq)
        y_at = _attention(q, kz, v, qi, kiz, wi, seq, top)
        w_out_p = jnp.concatenate([w_out[i][:2 * D_S5], _attn_out_rows(w_out[i][2 * D_S5:])], axis=0).astype(BF16)
        h1 = _outproj(y_scan.reshape(t, 2 * D_S5), y_at.reshape(t, D_ATTN), h3.reshape(t, D_MODEL),
                      w_out_p, row(ln1_g[i]), row(ln1_b[i]))
        h2 = _ffn(h1, p[i].reshape(t, D_PLE),
                  ffn_w_up[i][:, :D_FF].astype(BF16), ffn_w_up[i][:, D_FF:].astype(BF16),
                  ffn_w_down[i].astype(BF16), ple_w_gate[i].astype(BF16), ple_w_proj[i].astype(BF16),
                  row(ln2_g[i]), row(ln2_b[i]))
        h3 = h2.reshape(BATCH, seq, D_MODEL)
    return h3


def kernel(x, p, positions, ln_emb_g, ln_emb_b, w_in, s5_lam_re, s5_lam_im, s5_log_step, s5_b_re, s5_b_im, s5_c_re, s5_c_im, s5_d, s5_w_glu, s5_b_glu, rg_conv_w, rg_conv_b, rg_wa, rg_ba, rg_wx, rg_bx, rg_lam, w_out, ln1_g, ln1_b, ffn_w_up, ffn_w_down, ple_w_gate, ple_w_proj, ln2_g, ln2_b):
    seq = x.shape[1]
    return _forward(seq, min(TOPK_MAX, seq // 4), x, p, positions, ln_emb_g, ln_emb_b, w_in,
                    s5_lam_re, s5_lam_im, s5_log_step, s5_b_re, s5_b_im, s5_c_re, s5_c_im,
                    s5_d, s5_w_glu, s5_b_glu,
                    rg_conv_w, rg_conv_b, rg_wa, rg_ba, rg_wx, rg_bx, rg_lam,
                    w_out, ln1_g, ln1_b, ffn_w_up, ffn_w_down, ple_w_gate, ple_w_proj,
                    ln2_g, ln2_b)
```

```python
import functools
import math

import jax
import jax.numpy as jnp
from jax import lax
from jax.experimental import pallas as pl
from jax.experimental.pallas import tpu as pltpu

F32 = jnp.float32
BF16 = jnp.bfloat16
I32 = jnp.int32

D_MODEL = 1024
BATCH = 8
SEQ = 2048
DEPTH = 2
D_S5 = 256
S5_GROUP = 16
S5_GROUPS = 16
S5_STATE = 64
N_S5_STATE = S5_GROUPS * S5_STATE
D_RG = 256
RG_BLOCKS = 8
RG_BLOCK = 32
RG_CONV = 4
RG_C = 8.0
N_HEADS = 8
N_KV_HEADS = 2
HEAD_DIM = 64
D_ATTN = 512
KV_DIM = 128
IDX_HEADS = 8
IDX_DIM = 64
TOPK_MAX = 256
D_MIX = 1024
ROPE_THETA = 10000.0
D_FF = 2816
D_PLE = 256
ALPHA = (2.0 * DEPTH) ** 0.25
LN_EPS = 1e-5
N_IN = 2120
N_IN_PAD = 2176
ATT_SCALE = HEAD_DIM ** -0.5
IDX_SCALE = (IDX_HEADS * IDX_DIM) ** -0.5

LANES = 128
SUBLANES = 8
PACK16 = 16
HALF16 = 32768
KV_CLASS = 512
ATT_CHUNK = 256
NEG_INF_KEY = (0xFF800000 ^ 0x7FFFFFFF) - 2 ** 32

VMEM_LIMIT = 56 * 1024 * 1024


def _dot(a, b):
    return jnp.dot(a, b, preferred_element_type=F32)


def _dot_nt(a, b):
    return lax.dot_general(a, b, (((1,), (1,)), ((), ())), preferred_element_type=F32)


def _ln(x, g, b):
    mu = jnp.mean(x, axis=-1, keepdims=True)
    xc = x - mu
    var = jnp.mean(xc * xc, axis=-1, keepdims=True)
    return xc * lax.rsqrt(var + LN_EPS) * g + b


def _gelu(x):
    c = math.sqrt(2.0 / math.pi)
    return 0.5 * x * (1.0 + jnp.tanh(c * (x + 0.044715 * (x * x * x))))


def _sigmoid(x):
    return 1.0 / (1.0 + jnp.exp(-x))


def _full_spec(shape):
    n = len(shape)
    return pl.BlockSpec(shape, lambda *_: (0,) * n)


def _rope_table_kernel(pos_ref, inv_ref, sgn_ref, cos_ref, sin_ref):
    ang = pos_ref[...].astype(F32) * inv_ref[...]
    cos_ref[...] = jnp.cos(ang)
    sin_ref[...] = jnp.sin(ang) * sgn_ref[...]


def _rope_tables(positions, seq):
    t = BATCH * seq
    tm = min(t, 2048)
    inv = ROPE_THETA ** (-jnp.arange(0, HEAD_DIM, 2, dtype=F32) / HEAD_DIM)
    inv128 = jnp.tile(inv, 4)[None, :]
    sgn = jnp.where((jnp.arange(LANES) % HEAD_DIM) < HEAD_DIM // 2, -1.0, 1.0).astype(F32)[None, :]
    pos = positions.reshape(t, 1)
    cos, sin = pl.pallas_call(
        _rope_table_kernel,
        out_shape=(jax.ShapeDtypeStruct((t, LANES), F32),) * 2,
        grid=(t // tm,),
        in_specs=[pl.BlockSpec((tm, 1), lambda i: (i, 0)), _full_spec((1, LANES)), _full_spec((1, LANES))],
        out_specs=(pl.BlockSpec((tm, LANES), lambda i: (i, 0)),) * 2,
        compiler_params=pltpu.CompilerParams(dimension_semantics=("parallel",)),
        name="rope_tables",
    )(pos, inv128, sgn)
    return cos.reshape(BATCH, seq, LANES), sin.reshape(BATCH, seq, LANES)


def _inproj_kernel(apply_ln, tt, h_ref, w_ref, cos_ref, sin_ref, g_ref, b_ref, *outs):
    if apply_ln:
        hn_ref, so_ref, q_ref, kz_ref, v_ref, qi_ref, kiz_ref, wi_ref = outs
    else:
        so_ref, q_ref, kz_ref, v_ref, qi_ref, kiz_ref, wi_ref = outs
    rows = BATCH * tt
    h = h_ref[...].reshape(rows, D_MODEL)
    if apply_ln:
        h = _ln(h, g_ref[...], b_ref[...])
        hn_ref[...] = h.reshape(BATCH, tt, D_MODEL)
    hb = h.astype(BF16)
    cos = cos_ref[...].reshape(rows, LANES)
    sin = sin_ref[...].reshape(rows, LANES)
    lane = lax.broadcasted_iota(I32, (rows, LANES), 1)
    first_half = (lane & (HEAD_DIM // 2)) == 0
    lo64 = lane < HEAD_DIM

    def rope(x, cs, sn):
        partner = jnp.where(first_half, pltpu.roll(x, LANES - 32, 1), pltpu.roll(x, 32, 1))
        return x * cs + partner * sn

    def rope_wide(x, n_chunks):
        return jnp.concatenate(
            [rope(x[:, c * LANES:(c + 1) * LANES], cos, sin) for c in range(n_chunks)], axis=1)

    ps = _dot(hb, w_ref[:, 0:768])
    for j in range(6):
        for b in range(BATCH):
            so_ref[j, pl.ds(b, tt, stride=BATCH), :] = ps[b * tt:(b + 1) * tt, j * LANES:(j + 1) * LANES]

    pq = _dot(hb, w_ref[:, 768:1280])
    q_ref[...] = (rope_wide(pq, 4) * ATT_SCALE).astype(BF16).reshape(BATCH, tt, D_ATTN)

    pkv = _dot(hb, w_ref[:, 1280:1536])
    kr = rope(pkv[:, 0:LANES], cos, sin)
    ksw = pltpu.roll(kr, HEAD_DIM, 1)
    zero = jnp.zeros_like(kr)
    kz = jnp.concatenate([jnp.where(lo64, kr, zero), jnp.where(lo64, zero, ksw),
                          jnp.where(lo64, ksw, zero), jnp.where(lo64, zero, kr)], axis=1)
    kz_ref[...] = kz.astype(BF16).reshape(BATCH, tt, 4 * LANES)
    v_ref[...] = pkv[:, LANES:2 * LANES].astype(BF16).reshape(BATCH, tt, KV_DIM)

    pqi = _dot(hb, w_ref[:, 1536:2048])
    qi_ref[...] = rope_wide(pqi, 4).astype(BF16).reshape(BATCH, tt, IDX_HEADS * IDX_DIM)

    pk = _dot(hb, w_ref[:, 2048:N_IN_PAD])
    kir = rope(pk, jnp.where(lo64, cos, 1.0), jnp.where(lo64, sin, 0.0))
    kie = jnp.where(lo64, kir, zero)
    kiz = jnp.concatenate([kie, pltpu.roll(kie, HEAD_DIM, 1)], axis=1)
    kiz_ref[...] = kiz.astype(BF16).reshape(BATCH, tt, 2 * LANES)
    wi_ref[...] = pk.reshape(BATCH, tt, LANES)


def _inproj(h3, w_in_p, cos, sin, ln_g, ln_b, apply_ln, seq):
    tt = min(seq, 128)
    nt = seq // tt
    blk = lambda c: pl.BlockSpec((BATCH, tt, c), lambda i: (0, i, 0))
    out_shape = [
        jax.ShapeDtypeStruct((6, seq * BATCH, LANES), F32),
        jax.ShapeDtypeStruct((BATCH, seq, D_ATTN), BF16),
        jax.ShapeDtypeStruct((BATCH, seq, 4 * LANES), BF16),
        jax.ShapeDtypeStruct((BATCH, seq, KV_DIM), BF16),
        jax.ShapeDtypeStruct((BATCH, seq, IDX_HEADS * IDX_DIM), BF16),
        jax.ShapeDtypeStruct((BATCH, seq, 2 * LANES), BF16),
        jax.ShapeDtypeStruct((BATCH, seq, LANES), F32),
    ]
    out_specs = [pl.BlockSpec((6, tt * BATCH, LANES), lambda i: (0, i, 0)),
                 blk(D_ATTN), blk(4 * LANES), blk(KV_DIM), blk(512), blk(2 * LANES), blk(LANES)]
    if apply_ln:
        out_shape = [jax.ShapeDtypeStruct((BATCH, seq, D_MODEL), F32)] + out_shape
        out_specs = [blk(D_MODEL)] + out_specs
    return pl.pallas_call(
        functools.partial(_inproj_kernel, apply_ln, tt),
        out_shape=tuple(out_shape),
        grid=(nt,),
        in_specs=[blk(D_MODEL), _full_spec((D_MODEL, N_IN_PAD)), blk(LANES), blk(LANES),
                  _full_spec((1, D_MODEL)), _full_spec((1, D_MODEL))],
        out_specs=tuple(out_specs),
        compiler_params=pltpu.CompilerParams(dimension_semantics=("parallel",),
                                             vmem_limit_bytes=VMEM_LIMIT),
        name="inproj_ln" if apply_ln else "inproj",
    )(h3, w_in_p, cos, sin, ln_g, ln_b)


def _s5_param_kernel(lr_ref, li_ref, ls_ref, br_ref, bi_ref, a_ref, bd_ref):
    lr = lr_ref[...]
    li = li_ref[...]
    step = jnp.exp(ls_ref[...])
    mag = jnp.exp(lr * step)
    ar = mag * jnp.cos(li * step)
    ai = mag * jnp.sin(li * step)
    den = lr * lr + li * li
    nr, ni = ar - 1.0, ai
    cr = (nr * lr + ni * li) / den
    ci = (ni * lr - nr * li) / den
    br = br_ref[...]
    bi = bi_ref[...]
    bbr = cr * br - ci * bi
    bbi = cr * bi + ci * br
    row = lax.broadcasted_iota(I32, (D_S5, N_S5_STATE), 0)
    col = lax.broadcasted_iota(I32, (D_S5, N_S5_STATE), 1)
    blk = (row // S5_GROUP) == (col // S5_STATE)
    zero = jnp.zeros((D_S5, N_S5_STATE), F32)
    bd_ref[:, 0:N_S5_STATE] = jnp.where(blk, jnp.concatenate([bbr] * S5_GROUPS, axis=0), zero).astype(BF16)
    bd_ref[:, N_S5_STATE:] = jnp.where(blk, jnp.concatenate([bbi] * S5_GROUPS, axis=0), zero).astype(BF16)
    a_ref[:, 0:N_S5_STATE] = jnp.broadcast_to(ar, (SUBLANES, N_S5_STATE))
    a_ref[:, N_S5_STATE:] = jnp.broadcast_to(ai, (SUBLANES, N_S5_STATE))


def _s5_params(lam_re, lam_im, log_step, b_re, b_im):
    lr = lam_re.reshape(1, N_S5_STATE)
    li = lam_im.reshape(1, N_S5_STATE)
    ls = jnp.repeat(log_step, S5_STATE).reshape(1, N_S5_STATE)
    br = b_re.transpose(2, 0, 1).reshape(S5_GROUP, N_S5_STATE)
    bi = b_im.transpose(2, 0, 1).reshape(S5_GROUP, N_S5_STATE)
    return pl.pallas_call(
        _s5_param_kernel,
        out_shape=(jax.ShapeDtypeStruct((SUBLANES, 2 * N_S5_STATE), F32),
                   jax.ShapeDtypeStruct((D_S5, 2 * N_S5_STATE), BF16)),
        name="s5_params",
    )(lr, li, ls, br, bi)


def _scan_kernel(tt, si_ref, a_ref, bd_ref, cd_ref, dsk_ref, wglu_ref, bglu_ref,
                 cw_ref, cb_ref, wa_ref, ba_ref, wx_ref, bx_ref, lam_ref,
                 o_ref, x_ref, hs_ref, halo_ref, ab_ref, hrg_ref, y_ref):
    rows = BATCH * tt
    i = pl.program_id(0)

    @pl.when(i == 0)
    def _init():
        hs_ref[...] = jnp.zeros_like(hs_ref)
        halo_ref[...] = jnp.zeros_like(halo_ref)
        hrg_ref[...] = jnp.zeros_like(hrg_ref)

    u = jnp.concatenate([si_ref[0], si_ref[1]], axis=1)
    x_ref[...] = _dot(u.astype(BF16), bd_ref[...])
    half = N_S5_STATE // 2
    for hh in range(2):
        lo = hh * half
        ar = a_ref[:, lo:lo + half]
        ai = a_ref[:, N_S5_STATE + lo:N_S5_STATE + lo + half]

        def step(t, carry, lo=lo, ar=ar, ai=ai):
            hr, hi = carry
            r0 = pl.multiple_of(t * BATCH, BATCH)
            br = x_ref[pl.ds(r0, BATCH), lo:lo + half]
            bi = x_ref[pl.ds(r0, BATCH), N_S5_STATE + lo:N_S5_STATE + lo + half]
            nr = ar * hr - ai * hi + br
            ni = ar * hi + ai * hr + bi
            x_ref[pl.ds(r0, BATCH), lo:lo + half] = nr
            x_ref[pl.ds(r0, BATCH), N_S5_STATE + lo:N_S5_STATE + lo + half] = ni
            return nr, ni

        hr, hi = lax.fori_loop(
            0, tt, step,
            (hs_ref[:, lo:lo + half], hs_ref[:, N_S5_STATE + lo:N_S5_STATE + lo + half]),
            unroll=4)
        hs_ref[:, lo:lo + half] = hr
        hs_ref[:, N_S5_STATE + lo:N_S5_STATE + lo + half] = hi

    y = _dot(x_ref[...].astype(BF16), cd_ref[...]) + dsk_ref[...] * u
    y = _gelu(y)
    y = y * _sigmoid(_dot(y.astype(BF16), wglu_ref[...]) + bglu_ref[...])
    y_ref[0] = y[:, 0:LANES]
    y_ref[1] = y[:, LANES:2 * LANES]

    xr = jnp.concatenate([si_ref[2], si_ref[3]], axis=1)
    gate = jnp.concatenate([si_ref[4], si_ref[5]], axis=1)
    hal = (RG_CONV - 1) * BATCH
    xext = jnp.concatenate([halo_ref[...], xr], axis=0)
    halo_ref[...] = xr[rows - hal:rows, :]
    xc = cb_ref[...]
    for k in range(RG_CONV):
        xc = xc + cw_ref[k:k + 1, :] * xext[k * BATCH:k * BATCH + rows, :]
    xcb = xc.astype(BF16)
    r = _sigmoid(_dot(xcb, wa_ref[...]) + ba_ref[...])
    ig = _sigmoid(_dot(xcb, wx_ref[...]) + bx_ref[...])
    nl = -lam_ref[...]
    softplus = jnp.maximum(nl, 0.0) + jnp.log(1.0 + jnp.exp(-jnp.abs(nl)))
    log_a = -RG_C * r * softplus
    a = jnp.exp(log_a)
    mult = jnp.sqrt(1.0 - a * a)
    ab_ref[0] = a
    ab_ref[1] = mult * (ig * xc)

    def rg_step(t, h):
        r0 = pl.multiple_of(t * BATCH, BATCH)
        hn = ab_ref[0, pl.ds(r0, BATCH), :] * h + ab_ref[1, pl.ds(r0, BATCH), :]
        ab_ref[1, pl.ds(r0, BATCH), :] = hn
        return hn

    hrg_ref[...] = lax.fori_loop(0, tt, rg_step, hrg_ref[...], unroll=8)
    yr = ab_ref[1] * _gelu(gate)
    y_ref[2] = yr[:, 0:LANES]
    y_ref[3] = yr[:, LANES:2 * LANES]

    for b in range(BATCH):
        o_ref[b] = jnp.concatenate(
            [y_ref[j, pl.ds(b, tt, stride=BATCH), :] for j in range(4)], axis=1).astype(BF16)


def _scans(scan_in, a8, bd, cd, dsk, wglu, bglu, cw, cb, wa, ba, wx, bx, lam, seq):
    tt = min(seq, 64)
    nt = seq // tt
    rows = tt * BATCH
    params = [a8, bd, cd, dsk, wglu, bglu, cw, cb, wa, ba, wx, bx, lam]
    return pl.pallas_call(
        functools.partial(_scan_kernel, tt),
        out_shape=jax.ShapeDtypeStruct((BATCH, seq, 2 * D_S5), BF16),
        grid=(nt,),
        in_specs=[pl.BlockSpec((6, rows, LANES), lambda i: (0, i, 0))] + [_full_spec(p.shape) for p in params],
        out_specs=pl.BlockSpec((BATCH, tt, 2 * D_S5), lambda i: (0, i, 0)),
        scratch_shapes=[
            pltpu.VMEM((rows, 2 * N_S5_STATE), F32),
            pltpu.VMEM((BATCH, 2 * N_S5_STATE), F32),
            pltpu.VMEM(((RG_CONV - 1) * BATCH, D_RG), F32),
            pltpu.VMEM((2, rows, D_RG), F32),
            pltpu.VMEM((BATCH, D_RG), F32),
            pltpu.VMEM((4, rows, LANES), F32),
        ],
        compiler_params=pltpu.CompilerParams(dimension_semantics=("arbitrary",),
                                             vmem_limit_bytes=VMEM_LIMIT),
        name="scans",
    )(scan_in, *params)


def _count16(ref, skv, cand, op):
    cb = jnp.broadcast_to(cand, (PACK16, LANES))
    accs = []
    for i in range(skv // PACK16):
        one = jnp.where(op(ref[i * PACK16:(i + 1) * PACK16, :], cb), jnp.int16(1), jnp.int16(0))
        if i < 4:
            accs.append(one)
        else:
            accs[i % 4] = accs[i % 4] + one
    tot = (accs[0] + accs[1]) + (accs[2] + accs[3])
    return jnp.sum(tot.astype(I32), axis=0, keepdims=True)


def _pipelined(nch, mm, post, carry):
    mm(0, 0)

    def body(i, carry):
        c0 = 2 * i
        mm(c0 + 1, 1)
        carry = post(c0, 0, carry)
        mm(jnp.minimum(c0 + 2, nch - 2), 0)
        return post(c0 + 1, 1, carry)

    return lax.fori_loop(0, nch // 2, body, carry)


def _attn_kernel(seq, top, q_ref, kz_ref, v_ref, qi_ref, kiz_ref, wi_ref, o_ref,
                 vt_ref, key_ref, hi_ref, lo_ref, lom_ref, tie_ref, bias_ref, mm_ref, l_ref, p_ref):
    j = pl.program_id(1)

    @pl.when(j == 0)
    def _():
        vt_ref[...] = v_ref[...].astype(F32).T.astype(BF16)

    cls_rows = min(seq, KV_CLASS)
    for c in range(seq // cls_rows):
        @pl.when(j // (cls_rows // LANES) == c)
        def _(c=c):
            _attn_class(cls_rows * (c + 1), top, j, q_ref, kz_ref, qi_ref, kiz_ref, wi_ref, o_ref,
                        vt_ref, key_ref, hi_ref, lo_ref, lom_ref, tie_ref, bias_ref, mm_ref, l_ref, p_ref)


def _attn_class(skv, top, j, q_ref, kz_ref, qi_ref, kiz_ref, wi_ref, o_ref,
                vt_ref, key_ref, hi_ref, lo_ref, lom_ref, tie_ref, bias_ref, mm_ref, l_ref, p_ref):
    ck = ATT_CHUNK
    nc = skv // ck
    qb = LANES
    t_idx = j * qb + lax.broadcasted_iota(I32, (ck, qb), 1)
    s_iota = lax.broadcasted_iota(I32, (ck, qb), 0)

    wts = wi_ref[...].T[HEAD_DIM:HEAD_DIM + IDX_HEADS, :] * IDX_SCALE
    qi = qi_ref[...]
    qs = jnp.concatenate([qi[:, c * LANES:(c + 1) * LANES] for c in range(4)], axis=0)

    def mm_idx(c, slot):
        r0 = pl.multiple_of(c * ck, ck)
        mm_ref[slot, :, 0:4 * qb] = _dot_nt(kiz_ref[pl.ds(r0, ck), 0:LANES], qs)
        mm_ref[slot, :, 4 * qb:8 * qb] = _dot_nt(kiz_ref[pl.ds(r0, ck), LANES:2 * LANES], qs)

    def post_idx(c, slot, carry):
        r0 = pl.multiple_of(c * ck, ck)
        acc = None
        for c4 in range(4):
            te = wts[2 * c4:2 * c4 + 1, :] * jnp.maximum(mm_ref[slot, :, c4 * qb:(c4 + 1) * qb], 0.0)
            to = wts[2 * c4 + 1:2 * c4 + 2, :] * jnp.maximum(mm_ref[slot, :, (4 + c4) * qb:(5 + c4) * qb], 0.0)
            acc = te + to if acc is None else acc + (te + to)
        acc = jnp.where(r0 + s_iota <= t_idx, acc, -jnp.inf)
        bits = pltpu.bitcast(acc, I32)
        key = jnp.where(bits < 0, bits ^ 0x7FFFFFFF, bits)
        key_ref[pl.ds(r0, ck), :] = key
        hi_ref[pl.ds(r0, ck), :] = jnp.right_shift(key, 16).astype(jnp.int16)
        lo_ref[pl.ds(r0, ck), :] = ((key & 0xFFFF) - HALF16).astype(jnp.int16)
        return carry

    _pipelined(nc, mm_idx, post_idx, 0)

    ge = lambda a, b: a >= b
    gt = lambda a, b: a > b
    lt = lambda a, b: a < b

    def search16(ref, need):
        def body(i, tu):
            cand = tu | jnp.left_shift(jnp.int32(1), 15 - i)
            cnt = _count16(ref, skv, (cand - HALF16).astype(jnp.int16), ge)
            return jnp.where(cnt >= need, cand, tu)
        return lax.fori_loop(0, 16, body, jnp.zeros((1, qb), I32))

    thi_u = search16(hi_ref, top)
    thi = (thi_u - HALF16).astype(jnp.int16)
    n_gt_hi = _count16(hi_ref, skv, thi, gt)
    thi_b = jnp.broadcast_to(thi, (PACK16, qb))
    for i in range(skv // PACK16):
        rs = slice(i * PACK16, (i + 1) * PACK16)
        lom_ref[rs, :] = jnp.where(hi_ref[rs, :] == thi_b, lo_ref[rs, :], jnp.int16(-HALF16))
    tlo_u = search16(lom_ref, top - n_gt_hi)
    thr = jnp.left_shift(thi_u - HALF16, 16) | tlo_u

    def count32(ref, cand, op):
        def body(c, acc):
            r0 = pl.multiple_of(c * ck, ck)
            m = op(ref[pl.ds(r0, ck), :], cand)
            return acc + jnp.sum(jnp.where(m, 1, 0).astype(I32).reshape(ck // SUBLANES, SUBLANES, qb), axis=0)
        acc = lax.fori_loop(0, nc, body, jnp.zeros((SUBLANES, qb), I32))
        return jnp.sum(acc, axis=0, keepdims=True)

    def bias_default(c, acc):
        r0 = pl.multiple_of(c * ck, ck)
        m = key_ref[pl.ds(r0, ck), :] >= thr
        sel = jnp.logical_and(m, r0 + s_iota <= t_idx)
        bias_ref[pl.ds(r0, ck), :] = jnp.where(sel, 0.0, -jnp.inf).astype(F32)
        return acc + jnp.sum(jnp.where(m, 1, 0).astype(I32).reshape(ck // SUBLANES, SUBLANES, qb), axis=0)

    n_ge = jnp.sum(lax.fori_loop(0, nc, bias_default, jnp.zeros((SUBLANES, qb), I32)), axis=0, keepdims=True)
    tie = jnp.logical_and(n_ge > top, thr != NEG_INF_KEY)
    any_tie = jnp.max(jnp.where(tie, 1, 0).astype(I32)) > 0

    @pl.when(any_tie)
    def _tie():
        nbits = (skv - 1).bit_length()
        need = top - count32(key_ref, thr, gt)
        big = jnp.int32(2 ** nbits)

        def fill(c, carry):
            r0 = pl.multiple_of(c * ck, ck)
            k = key_ref[pl.ds(r0, ck), :]
            tie_ref[pl.ds(r0, ck), :] = jnp.where(k == thr, r0 + s_iota, big)
            return carry

        lax.fori_loop(0, nc, fill, 0)

        def bit2(i, m):
            cand = m | jnp.left_shift(jnp.int32(1), nbits - 1 - i)
            cnt = count32(tie_ref, cand, lt)
            return jnp.where(cnt < need, cand, m)

        m = lax.fori_loop(0, nbits, bit2, jnp.zeros((1, qb), I32))

        def bias_tie(c, carry):
            r0 = pl.multiple_of(c * ck, ck)
            k = key_ref[pl.ds(r0, ck), :]
            sel = jnp.logical_or(k > thr, tie_ref[pl.ds(r0, ck), :] <= m)
            sel = jnp.logical_and(sel, r0 + s_iota <= t_idx)
            bias_ref[pl.ds(r0, ck), :] = jnp.where(sel, 0.0, -jnp.inf).astype(F32)
            return carry

        lax.fori_loop(0, nc, bias_tie, 0)

    q = q_ref[...]
    o_parts = []
    for g in range(N_KV_HEADS):
        qe = jnp.concatenate([q[:, (2 * g) * LANES:(2 * g + 1) * LANES],
                              q[:, (2 * g + 1) * LANES:(2 * g + 2) * LANES]], axis=0)

        def mm_l(c, slot, g=g, qe=qe):
            r0 = pl.multiple_of(c * ck, ck)
            mm_ref[slot, :, 0:2 * qb] = _dot_nt(kz_ref[pl.ds(r0, ck), (2 * g) * LANES:(2 * g + 1) * LANES], qe)
            mm_ref[slot, :, 2 * qb:4 * qb] = _dot_nt(kz_ref[pl.ds(r0, ck), (2 * g + 1) * LANES:(2 * g + 2) * LANES], qe)

        def post_l(c, slot, m8):
            r0 = pl.multiple_of(c * ck, ck)
            b = bias_ref[pl.ds(r0, ck), :]
            l = mm_ref[slot, :, 0:4 * qb] + jnp.concatenate([b] * 4, axis=1)
            l_ref[pl.ds(r0, ck), :] = l
            return jnp.maximum(m8, jnp.max(l.reshape(ck // SUBLANES, SUBLANES, 4 * qb), axis=0))

        m8 = _pipelined(nc, mm_l, post_l, jnp.full((SUBLANES, 4 * qb), -jnp.inf, F32))
        mx = jnp.max(m8, axis=0, keepdims=True)

        def pbody(c, s8, mx=mx):
            r0 = pl.multiple_of(c * ck, ck)
            p = jnp.exp(l_ref[pl.ds(r0, ck), :] - mx)
            p_ref[pl.ds(r0, ck), :] = p.astype(BF16)
            return s8 + jnp.sum(p.reshape(ck // SUBLANES, SUBLANES, 4 * qb), axis=0)

        s8 = lax.fori_loop(0, nc, pbody, jnp.zeros((SUBLANES, 4 * qb), F32))
        denom = jnp.sum(s8, axis=0, keepdims=True)
        ot = _dot(vt_ref[g * HEAD_DIM:(g + 1) * HEAD_DIM, 0:skv], p_ref[0:skv, :])
        o_parts.append(ot * (1.0 / denom))
    ot = jnp.concatenate(o_parts, axis=0)
    y = jnp.concatenate([ot[:, i * qb:(i + 1) * qb].T for i in range(4)], axis=1)
    o_ref[...] = y.astype(BF16)


def _attention(q, kz, v, qi, kiz, wi, seq, top):
    qb = LANES
    sq = pl.Squeezed()
    per_q = lambda c: pl.BlockSpec((sq, qb, c), lambda b, j: (b, j, 0))
    per_b = lambda c: pl.BlockSpec((sq, seq, c), lambda b, j: (b, 0, 0))
    return pl.pallas_call(
        functools.partial(_attn_kernel, seq, top),
        out_shape=jax.ShapeDtypeStruct((BATCH, seq, D_ATTN), BF16),
        grid=(BATCH, seq // qb),
        in_specs=[per_q(D_ATTN), per_b(4 * LANES), per_b(KV_DIM), per_q(512), per_b(2 * LANES), per_q(LANES)],
        out_specs=per_q(D_ATTN),
        scratch_shapes=[
            pltpu.VMEM((KV_DIM, seq), BF16),
            pltpu.VMEM((seq, qb), I32),
            pltpu.VMEM((seq, qb), jnp.int16),
            pltpu.VMEM((seq, qb), jnp.int16),
            pltpu.VMEM((seq, qb), jnp.int16),
            pltpu.VMEM((seq, qb), I32),
            pltpu.VMEM((seq, qb), F32),
            pltpu.VMEM((2, ATT_CHUNK, 8 * qb), F32),
            pltpu.VMEM((seq, 4 * qb), F32),
            pltpu.VMEM((seq, 4 * qb), BF16),
        ],
        compiler_params=pltpu.CompilerParams(dimension_semantics=("parallel", "arbitrary"),
                                             vmem_limit_bytes=VMEM_LIMIT),
        name="dsa_attention",
    )(q, kz, v, qi, kiz, wi)


def _outproj_kernel(ys_ref, ya_ref, h_ref, w_ref, g_ref, b_ref, o_ref):
    mix = _dot(ys_ref[...], w_ref[0:2 * D_S5, :]) + _dot(ya_ref[...], w_ref[2 * D_S5:, :])
    o_ref[...] = _ln(ALPHA * h_ref[...] + mix, g_ref[...], b_ref[...])


def _outproj(y_scan, y_at, h, w_out_p, g, b):
    t = h.shape[0]
    tm = min(t, 512)
    row = lambda c: pl.BlockSpec((tm, c), lambda i: (i, 0))
    return pl.pallas_call(
        _outproj_kernel,
        out_shape=jax.ShapeDtypeStruct((t, D_MODEL), F32),
        grid=(t // tm,),
        in_specs=[row(2 * D_S5), row(D_ATTN), row(D_MODEL), _full_spec((D_MIX, D_MODEL)),
                  _full_spec((1, D_MODEL)), _full_spec((1, D_MODEL))],
        out_specs=row(D_MODEL),
        compiler_params=pltpu.CompilerParams(dimension_semantics=("parallel",),
                                             vmem_limit_bytes=VMEM_LIMIT),
        name="outproj_ln",
    )(y_scan, y_at, h, w_out_p, g, b)


def _ffn_kernel(h_ref, p_ref, wg_ref, wu_ref, wd_ref, wpg_ref, wpp_ref, g_ref, b_ref, o_ref):
    h = h_ref[...]
    hb = h.astype(BF16)
    gate = _dot(hb, wg_ref[...])
    up = _dot(hb, wu_ref[...])
    act = (gate * _sigmoid(gate) * up).astype(BF16)
    ffn = _dot(act, wd_ref[...])
    ple = _sigmoid(_dot(hb, wpg_ref[...])) * _dot(p_ref[...].astype(BF16), wpp_ref[...])
    o_ref[...] = _ln(ALPHA * h + ffn + ple, g_ref[...], b_ref[...])


def _ffn(h, p, wg, wu, wd, wpg, wpp, g, b):
    t = h.shape[0]
    tm = min(t, 512)
    row = lambda c: pl.BlockSpec((tm, c), lambda i: (i, 0))
    const = lambda shape: pl.BlockSpec(shape, lambda i: (0, 0), pipeline_mode=pl.Buffered(1))
    return pl.pallas_call(
        _ffn_kernel,
        out_shape=jax.ShapeDtypeStruct((t, D_MODEL), F32),
        grid=(t // tm,),
        in_specs=[row(D_MODEL), row(D_PLE), const((D_MODEL, D_FF)), const((D_MODEL, D_FF)),
                  const((D_FF, D_MODEL)), const((D_MODEL, D_MODEL)), const((D_PLE, D_MODEL)),
                  _full_spec((1, D_MODEL)), _full_spec((1, D_MODEL))],
        out_specs=row(D_MODEL),
        compiler_params=pltpu.CompilerParams(dimension_semantics=("parallel",),
                                             vmem_limit_bytes=VMEM_LIMIT),
        name="ffn_ple_ln",
    )(h, p, wg, wu, wd, wpg, wpp, g, b)


def _block_diag(w):
    hh, n, m = w.shape
    eye = jnp.eye(hh, dtype=w.dtype)
    return jnp.einsum('hij,hk->hikj', w, eye).reshape(hh * n, hh * m)


def _s5_out_matrix(c_re, c_im):
    cre = _block_diag(c_re.transpose(0, 2, 1))
    cim = _block_diag(c_im.transpose(0, 2, 1))
    return jnp.concatenate([cre, -cim], axis=0).astype(BF16)


def _attn_out_rows(w_at):
    w = w_at.reshape(N_KV_HEADS, 4, HEAD_DIM, D_MODEL)
    w = w[:, jnp.array([0, 2, 1, 3])]
    return w.transpose(1, 0, 2, 3).reshape(D_ATTN, D_MODEL)


def _forward(seq, top, x, p, positions, ln_emb_g, ln_emb_b, w_in,
             s5_lam_re, s5_lam_im, s5_log_step, s5_b_re, s5_b_im, s5_c_re, s5_c_im,
             s5_d, s5_w_glu, s5_b_glu,
             rg_conv_w, rg_conv_b, rg_wa, rg_ba, rg_wx, rg_bx, rg_lam,
             w_out, ln1_g, ln1_b, ffn_w_up, ffn_w_down, ple_w_gate, ple_w_proj,
             ln2_g, ln2_b):
    t = BATCH * seq
    cos, sin = _rope_tables(positions, seq)
    row = lambda a: a.reshape(1, -1)
    h3 = x
    for i in range(DEPTH):
        w_in_p = jnp.pad(w_in[i], ((0, 0), (0, N_IN_PAD - N_IN))).astype(BF16)
        outs = _inproj(h3, w_in_p, cos, sin, row(ln_emb_g), row(ln_emb_b), i == 0, seq)
        if i == 0:
            h3, outs = outs[0], outs[1:]
        scan_in, q, kz, v, qi, kiz, wi = outs
        a8, bd = _s5_params(s5_lam_re[i], s5_lam_im[i], s5_log_step[i], s5_b_re[i], s5_b_im[i])
        y_scan = _scans(scan_in, a8, bd, _s5_out_matrix(s5_c_re[i], s5_c_im[i]), row(s5_d[i]),
                        s5_w_glu[i].astype(BF16), row(s5_b_glu[i]),
                        rg_conv_w[i], row(rg_conv_b[i]),
                        _block_diag(rg_wa[i]).astype(BF16), row(rg_ba[i]),
                        _block_diag(rg_wx[i]).astype(BF16), row(rg_bx[i]), row(rg_lam[i]), seq)
        y_at = _attention(q, kz, v, qi, kiz, wi, seq, top)
        w_out_p = jnp.concatenate([w_out[i][:2 * D_S5], _attn_out_rows(w_out[i][2 * D_S5:])], axis=0).astype(BF16)
        h1 = _outproj(y_scan.reshape(t, 2 * D_S5), y_at.reshape(t, D_ATTN), h3.reshape(t, D_MODEL),
                      w_out_p, row(ln1_g[i]), row(ln1_b[i]))
        h2 = _ffn(h1, p[i].reshape(t, D_PLE),
                  ffn_w_up[i][:, :D_FF].astype(BF16), ffn_w_up[i][:, D_FF:].astype(BF16),
                  ffn_w_down[i].astype(BF16), ple_w_gate[i].astype(BF16), ple_w_proj[i].astype(BF16),
                  row(ln2_g[i]), row(ln2_b[i]))
        h3 = h2.reshape(BATCH, seq, D_MODEL)
    return h3


def kernel(x, p, positions, ln_emb_g, ln_emb_b, w_in, s5_lam_re, s5_lam_im, s5_log_step, s5_b_re, s5_b_im, s5_c_re, s5_c_im, s5_d, s5_w_glu, s5_b_glu, rg_conv_w, rg_conv_b, rg_wa, rg_ba, rg_wx, rg_bx, rg_lam, w_out, ln1_g, ln1_b, ffn_w_up, ffn_w_down, ple_w_gate, ple_w_proj, ln2_g, ln2_b):
    seq = x.shape[1]
    return _forward(seq, min(TOPK_MAX, seq // 4), x, p, positions, ln_emb_g, ln_emb_b, w_in,
                    s5_lam_re, s5_lam_im, s5_log_step, s5_b_re, s5_b_im, s5_c_re, s5_c_im,
                    s5_d, s5_w_glu, s5_b_glu,
                    rg_conv_w, rg_conv_b, rg_wa, rg_ba, rg_wx, rg_bx, rg_lam,
                    w_out, ln1_g, ln1_b, ffn_w_up, ffn_w_down, ple_w_gate, ple_w_proj,
                    ln2_g, ln2_b)
```

```python
import functools
import math

import jax
import jax.numpy as jnp
from jax import lax
from jax.experimental import pallas as pl
from jax.experimental.pallas import tpu as pltpu

F32 = jnp.float32
BF16 = jnp.bfloat16
I32 = jnp.int32

D_MODEL = 1024
BATCH = 8
SEQ = 2048
DEPTH = 2
D_S5 = 256
S5_GROUP = 16
S5_GROUPS = 16
S5_STATE = 64
N_S5_STATE = S5_GROUPS * S5_STATE
D_RG = 256
RG_BLOCKS = 8
RG_BLOCK = 32
RG_CONV = 4
RG_C = 8.0
N_HEADS = 8
N_KV_HEADS = 2
HEAD_DIM = 64
D_ATTN = 512
KV_DIM = 128
IDX_HEADS = 8
IDX_DIM = 64
TOPK_MAX = 256
D_MIX = 1024
ROPE_THETA = 10000.0
D_FF = 2816
D_PLE = 256
ALPHA = (2.0 * DEPTH) ** 0.25
LN_EPS = 1e-5
N_IN = 2120
N_IN_PAD = 2176
ATT_SCALE = HEAD_DIM ** -0.5
IDX_SCALE = (IDX_HEADS * IDX_DIM) ** -0.5
LOG2E = math.log2(math.e)

LANES = 128
SUBLANES = 8
PACK16 = 16
HALF16 = 32768
KV_CLASS = 512
ATT_CHUNK = 256
ATT_MM_CHUNK = 512
ATT_SUB = 64
NEG_INF_KEY = (0xFF800000 ^ 0x7FFFFFFF) - 2 ** 32

VMEM_LIMIT = 56 * 1024 * 1024


def _dot(a, b):
    return jnp.dot(a, b, preferred_element_type=F32)


def _dot_nt(a, b):
    return lax.dot_general(a, b, (((1,), (1,)), ((), ())), preferred_element_type=F32)


def _ln(x, g, b):
    mu = jnp.mean(x, axis=-1, keepdims=True)
    xc = x - mu
    var = jnp.mean(xc * xc, axis=-1, keepdims=True)
    return xc * lax.rsqrt(var + LN_EPS) * g + b


def _gelu(x):
    c = math.sqrt(2.0 / math.pi)
    return 0.5 * x * (1.0 + jnp.tanh(c * (x + 0.044715 * (x * x * x))))


def _sigmoid(x):
    return 1.0 / (1.0 + jnp.exp(-x))


def _full_spec(shape):
    n = len(shape)
    return pl.BlockSpec(shape, lambda *_: (0,) * n)


def _rope_table_kernel(pos_ref, inv_ref, sgn_ref, cos_ref, sin_ref):
    ang = pos_ref[...].astype(F32) * inv_ref[...]
    cos_ref[...] = jnp.cos(ang)
    sin_ref[...] = jnp.sin(ang) * sgn_ref[...]


def _rope_tables(positions, seq):
    t = BATCH * seq
    tm = min(t, 2048)
    inv = ROPE_THETA ** (-jnp.arange(0, HEAD_DIM, 2, dtype=F32) / HEAD_DIM)
    inv128 = jnp.tile(inv, 4)[None, :]
    sgn = jnp.where((jnp.arange(LANES) % HEAD_DIM) < HEAD_DIM // 2, -1.0, 1.0).astype(F32)[None, :]
    pos = positions.reshape(t, 1)
    cos, sin = pl.pallas_call(
        _rope_table_kernel,
        out_shape=(jax.ShapeDtypeStruct((t, LANES), F32),) * 2,
        grid=(t // tm,),
        in_specs=[pl.BlockSpec((tm, 1), lambda i: (i, 0)), _full_spec((1, LANES)), _full_spec((1, LANES))],
        out_specs=(pl.BlockSpec((tm, LANES), lambda i: (i, 0)),) * 2,
        compiler_params=pltpu.CompilerParams(dimension_semantics=("parallel",)),
        name="rope_tables",
    )(pos, inv128, sgn)
    return cos.reshape(BATCH, seq, LANES), sin.reshape(BATCH, seq, LANES)


def _inproj_kernel(apply_ln, tt, h_ref, w_ref, cos_ref, sin_ref, g_ref, b_ref, *outs):
    if apply_ln:
        hn_ref, so_ref, q_ref, kz_ref, v_ref, qi_ref, kiz_ref, wi_ref = outs
    else:
        so_ref, q_ref, kz_ref, v_ref, qi_ref, kiz_ref, wi_ref = outs
    rows = BATCH * tt
    h = h_ref[...].reshape(rows, D_MODEL)
    if apply_ln:
        h = _ln(h, g_ref[...], b_ref[...])
        hn_ref[...] = h.reshape(BATCH, tt, D_MODEL)
    hb = h.astype(BF16)
    cos = cos_ref[...].reshape(rows, LANES)
    sin = sin_ref[...].reshape(rows, LANES)
    lane = lax.broadcasted_iota(I32, (rows, LANES), 1)
    first_half = (lane & (HEAD_DIM // 2)) == 0
    lo64 = lane < HEAD_DIM

    def rope(x, cs, sn):
        partner = jnp.where(first_half, pltpu.roll(x, LANES - 32, 1), pltpu.roll(x, 32, 1))
        return x * cs + partner * sn

    def rope_wide(x, n_chunks):
        return jnp.concatenate(
            [rope(x[:, c * LANES:(c + 1) * LANES], cos, sin) for c in range(n_chunks)], axis=1)

    ps = _dot(hb, w_ref[:, 0:768])
    for j in range(6):
        for b in range(BATCH):
            so_ref[j, pl.ds(b, tt, stride=BATCH), :] = ps[b * tt:(b + 1) * tt, j * LANES:(j + 1) * LANES]

    pq = _dot(hb, w_ref[:, 768:1280])
    q_ref[...] = (rope_wide(pq, 4) * (ATT_SCALE * LOG2E)).astype(BF16).reshape(BATCH, tt, D_ATTN)

    pkv = _dot(hb, w_ref[:, 1280:1536])
    kr = rope(pkv[:, 0:LANES], cos, sin)
    ksw = pltpu.roll(kr, HEAD_DIM, 1)
    zero = jnp.zeros_like(kr)
    kz = jnp.concatenate([jnp.where(lo64, kr, zero), jnp.where(lo64, zero, ksw),
                          jnp.where(lo64, ksw, zero), jnp.where(lo64, zero, kr)], axis=1)
    kz_ref[...] = kz.astype(BF16).reshape(BATCH, tt, 4 * LANES)
    v_ref[...] = pkv[:, LANES:2 * LANES].astype(BF16).reshape(BATCH, tt, KV_DIM)

    pqi = _dot(hb, w_ref[:, 1536:2048])
    qi_ref[...] = rope_wide(pqi, 4).astype(BF16).reshape(BATCH, tt, IDX_HEADS * IDX_DIM)

    pk = _dot(hb, w_ref[:, 2048:N_IN_PAD])
    kir = rope(pk, jnp.where(lo64, cos, 1.0), jnp.where(lo64, sin, 0.0))
    kie = jnp.where(lo64, kir, zero)
    kiz = jnp.concatenate([kie, pltpu.roll(kie, HEAD_DIM, 1)], axis=1)
    kiz_ref[...] = kiz.astype(BF16).reshape(BATCH, tt, 2 * LANES)
    wi_ref[...] = pk.reshape(BATCH, tt, LANES)


def _inproj(h3, w_in_p, cos, sin, ln_g, ln_b, apply_ln, seq):
    tt = min(seq, 128)
    nt = seq // tt
    blk = lambda c: pl.BlockSpec((BATCH, tt, c), lambda i: (0, i, 0))
    out_shape = [
        jax.ShapeDtypeStruct((6, seq * BATCH, LANES), F32),
        jax.ShapeDtypeStruct((BATCH, seq, D_ATTN), BF16),
        jax.ShapeDtypeStruct((BATCH, seq, 4 * LANES), BF16),
        jax.ShapeDtypeStruct((BATCH, seq, KV_DIM), BF16),
        jax.ShapeDtypeStruct((BATCH, seq, IDX_HEADS * IDX_DIM), BF16),
        jax.ShapeDtypeStruct((BATCH, seq, 2 * LANES), BF16),
        jax.ShapeDtypeStruct((BATCH, seq, LANES), F32),
    ]
    out_specs = [pl.BlockSpec((6, tt * BATCH, LANES), lambda i: (0, i, 0)),
                 blk(D_ATTN), blk(4 * LANES), blk(KV_DIM), blk(512), blk(2 * LANES), blk(LANES)]
    if apply_ln:
        out_shape = [jax.ShapeDtypeStruct((BATCH, seq, D_MODEL), F32)] + out_shape
        out_specs = [blk(D_MODEL)] + out_specs
    return pl.pallas_call(
        functools.partial(_inproj_kernel, apply_ln, tt),
        out_shape=tuple(out_shape),
        grid=(nt,),
        in_specs=[blk(D_MODEL), _full_spec((D_MODEL, N_IN_PAD)), blk(LANES), blk(LANES),
                  _full_spec((1, D_MODEL)), _full_spec((1, D_MODEL))],
        out_specs=tuple(out_specs),
        compiler_params=pltpu.CompilerParams(dimension_semantics=("parallel",),
                                             vmem_limit_bytes=VMEM_LIMIT),
        name="inproj_ln" if apply_ln else "inproj",
    )(h3, w_in_p, cos, sin, ln_g, ln_b)


def _s5_param_kernel(lr_ref, li_ref, ls_ref, br_ref, bi_ref, a_ref, bd_ref):
    lr = lr_ref[...]
    li = li_ref[...]
    step = jnp.exp(ls_ref[...])
    mag = jnp.exp(lr * step)
    ar = mag * jnp.cos(li * step)
    ai = mag * jnp.sin(li * step)
    den = lr * lr + li * li
    nr, ni = ar - 1.0, ai
    cr = (nr * lr + ni * li) / den
    ci = (ni * lr - nr * li) / den
    br = br_ref[...]
    bi = bi_ref[...]
    bbr = cr * br - ci * bi
    bbi = cr * bi + ci * br
    row = lax.broadcasted_iota(I32, (D_S5, N_S5_STATE), 0)
    col = lax.broadcasted_iota(I32, (D_S5, N_S5_STATE), 1)
    blk = (row // S5_GROUP) == (col // S5_STATE)
    zero = jnp.zeros((D_S5, N_S5_STATE), F32)
    bd_ref[:, 0:N_S5_STATE] = jnp.where(blk, jnp.concatenate([bbr] * S5_GROUPS, axis=0), zero).astype(BF16)
    bd_ref[:, N_S5_STATE:] = jnp.where(blk, jnp.concatenate([bbi] * S5_GROUPS, axis=0), zero).astype(BF16)
    a_ref[:, 0:N_S5_STATE] = jnp.broadcast_to(ar, (SUBLANES, N_S5_STATE))
    a_ref[:, N_S5_STATE:] = jnp.broadcast_to(ai, (SUBLANES, N_S5_STATE))


def _s5_params(lam_re, lam_im, log_step, b_re, b_im):
    lr = lam_re.reshape(1, N_S5_STATE)
    li = lam_im.reshape(1, N_S5_STATE)
    ls = jnp.repeat(log_step, S5_STATE).reshape(1, N_S5_STATE)
    br = b_re.transpose(2, 0, 1).reshape(S5_GROUP, N_S5_STATE)
    bi = b_im.transpose(2, 0, 1).reshape(S5_GROUP, N_S5_STATE)
    return pl.pallas_call(
        _s5_param_kernel,
        out_shape=(jax.ShapeDtypeStruct((SUBLANES, 2 * N_S5_STATE), F32),
                   jax.ShapeDtypeStruct((D_S5, 2 * N_S5_STATE), BF16)),
        name="s5_params",
    )(lr, li, ls, br, bi)


def _scan_kernel(tt, si_ref, a_ref, bd_ref, cd_ref, dsk_ref, wglu_ref, bglu_ref,
                 cw_ref, cb_ref, wa_ref, ba_ref, wx_ref, bx_ref, lam_ref,
                 o_ref, x_ref, hs_ref, halo_ref, ab_ref, hrg_ref, y_ref):
    rows = BATCH * tt
    i = pl.program_id(0)

    @pl.when(i == 0)
    def _init():
        hs_ref[...] = jnp.zeros_like(hs_ref)
        halo_ref[...] = jnp.zeros_like(halo_ref)
        hrg_ref[...] = jnp.zeros_like(hrg_ref)

    u = jnp.concatenate([si_ref[0], si_ref[1]], axis=1)
    x_ref[...] = _dot(u.astype(BF16), bd_ref[...])
    half = N_S5_STATE // 2
    for hh in range(2):
        lo = hh * half
        ar = a_ref[:, lo:lo + half]
        ai = a_ref[:, N_S5_STATE + lo:N_S5_STATE + lo + half]

        def step(t, carry, lo=lo, ar=ar, ai=ai):
            hr, hi = carry
            r0 = pl.multiple_of(t * BATCH, BATCH)
            br = x_ref[pl.ds(r0, BATCH), lo:lo + half]
            bi = x_ref[pl.ds(r0, BATCH), N_S5_STATE + lo:N_S5_STATE + lo + half]
            nr = ar * hr - ai * hi + br
            ni = ar * hi + ai * hr + bi
            x_ref[pl.ds(r0, BATCH), lo:lo + half] = nr
            x_ref[pl.ds(r0, BATCH), N_S5_STATE + lo:N_S5_STATE + lo + half] = ni
            return nr, ni

        hr, hi = lax.fori_loop(
            0, tt, step,
            (hs_ref[:, lo:lo + half], hs_ref[:, N_S5_STATE + lo:N_S5_STATE + lo + half]),
            unroll=True)
        hs_ref[:, lo:lo + half] = hr
        hs_ref[:, N_S5_STATE + lo:N_S5_STATE + lo + half] = hi

    y = _dot(x_ref[...].astype(BF16), cd_ref[...]) + dsk_ref[...] * u
    y = _gelu(y)
    y = y * _sigmoid(_dot(y.astype(BF16), wglu_ref[...]) + bglu_ref[...])
    y_ref[0] = y[:, 0:LANES]
    y_ref[1] = y[:, LANES:2 * LANES]

    xr = jnp.concatenate([si_ref[2], si_ref[3]], axis=1)
    gate = jnp.concatenate([si_ref[4], si_ref[5]], axis=1)
    hal = (RG_CONV - 1) * BATCH
    xext = jnp.concatenate([halo_ref[...], xr], axis=0)
    halo_ref[...] = xr[rows - hal:rows, :]
    xc = cb_ref[...]
    for k in range(RG_CONV):
        xc = xc + cw_ref[k:k + 1, :] * xext[k * BATCH:k * BATCH + rows, :]
    xcb = xc.astype(BF16)
    r = _sigmoid(_dot(xcb, wa_ref[...]) + ba_ref[...])
    ig = _sigmoid(_dot(xcb, wx_ref[...]) + bx_ref[...])
    nl = -lam_ref[...]
    softplus = jnp.maximum(nl, 0.0) + jnp.log(1.0 + jnp.exp(-jnp.abs(nl)))
    log_a = -RG_C * r * softplus
    a = jnp.exp(log_a)
    mult = jnp.sqrt(1.0 - a * a)
    ab_ref[0] = a
    ab_ref[1] = mult * (ig * xc)

    def rg_step(t, h):
        r0 = pl.multiple_of(t * BATCH, BATCH)
        hn = ab_ref[0, pl.ds(r0, BATCH), :] * h + ab_ref[1, pl.ds(r0, BATCH), :]
        ab_ref[1, pl.ds(r0, BATCH), :] = hn
        return hn

    hrg_ref[...] = lax.fori_loop(0, tt, rg_step, hrg_ref[...], unroll=True)
    yr = ab_ref[1] * _gelu(gate)
    y_ref[2] = yr[:, 0:LANES]
    y_ref[3] = yr[:, LANES:2 * LANES]

    for b in range(BATCH):
        o_ref[b] = jnp.concatenate(
            [y_ref[j, pl.ds(b, tt, stride=BATCH), :] for j in range(4)], axis=1).astype(BF16)


def _scans(scan_in, a8, bd, cd, dsk, wglu, bglu, cw, cb, wa, ba, wx, bx, lam, seq):
    tt = min(seq, 64)
    nt = seq // tt
    rows = tt * BATCH
    params = [a8, bd, cd, dsk, wglu, bglu, cw, cb, wa, ba, wx, bx, lam]
    return pl.pallas_call(
        functools.partial(_scan_kernel, tt),
        out_shape=jax.ShapeDtypeStruct((BATCH, seq, 2 * D_S5), BF16),
        grid=(nt,),
        in_specs=[pl.BlockSpec((6, rows, LANES), lambda i: (0, i, 0))] + [_full_spec(p.shape) for p in params],
        out_specs=pl.BlockSpec((BATCH, tt, 2 * D_S5), lambda i: (0, i, 0)),
        scratch_shapes=[
            pltpu.VMEM((rows, 2 * N_S5_STATE), F32),
            pltpu.VMEM((BATCH, 2 * N_S5_STATE), F32),
            pltpu.VMEM(((RG_CONV - 1) * BATCH, D_RG), F32),
            pltpu.VMEM((2, rows, D_RG), F32),
            pltpu.VMEM((BATCH, D_RG), F32),
            pltpu.VMEM((4, rows, LANES), F32),
        ],
        compiler_params=pltpu.CompilerParams(dimension_semantics=("arbitrary",),
                                             vmem_limit_bytes=VMEM_LIMIT),
        name="scans",
    )(scan_in, *params)


def _count16(ref, skv, cand, op):
    cb = jnp.broadcast_to(cand, (PACK16, LANES))
    accs = []
    for i in range(skv // PACK16):
        one = jnp.where(op(ref[i * PACK16:(i + 1) * PACK16, :], cb), jnp.int16(1), jnp.int16(0))
        if i < 4:
            accs.append(one)
        else:
            accs[i % 4] = accs[i % 4] + one
    tot = (accs[0] + accs[1]) + (accs[2] + accs[3])
    return jnp.sum(tot.astype(I32), axis=0, keepdims=True)


def _attn_kernel(seq, top, q_ref, kz_ref, v_ref, qi_ref, kiz_ref, wi_ref, o_ref,
                 vt_ref, key_ref, hi_ref, lo_ref, lom_ref, tie_ref, bias_ref, l_ref, p_ref):
    j = pl.program_id(1)

    @pl.when(j == 0)
    def _():
        vt = v_ref[...].astype(F32).T.astype(BF16)
        ones = jnp.where(lax.broadcasted_iota(I32, (PACK16, seq), 0) == 0, 1.0, 0.0).astype(BF16)
        for g in range(N_KV_HEADS):
            vt_ref[g, 0:HEAD_DIM, :] = vt[g * HEAD_DIM:(g + 1) * HEAD_DIM, :]
            vt_ref[g, HEAD_DIM:HEAD_DIM + PACK16, :] = ones

    cls_rows = min(seq, KV_CLASS)
    for c in range(seq // cls_rows):
        @pl.when(j // (cls_rows // LANES) == c)
        def _(c=c):
            _attn_class(cls_rows * (c + 1), top, j, q_ref, kz_ref, qi_ref, kiz_ref, wi_ref, o_ref,
                        vt_ref, key_ref, hi_ref, lo_ref, lom_ref, tie_ref, bias_ref, l_ref, p_ref)


def _attn_class(skv, top, j, q_ref, kz_ref, qi_ref, kiz_ref, wi_ref, o_ref,
                vt_ref, key_ref, hi_ref, lo_ref, lom_ref, tie_ref, bias_ref, l_ref, p_ref):
    ck = ATT_CHUNK
    nc = skv // ck
    mk = min(skv, ATT_MM_CHUNK)
    sub = ATT_SUB
    qb = LANES
    t_idx = j * qb + lax.broadcasted_iota(I32, (ck, qb), 1)
    s_iota = lax.broadcasted_iota(I32, (ck, qb), 0)
    t_sub = j * qb + lax.broadcasted_iota(I32, (sub, qb), 1)
    s_sub = lax.broadcasted_iota(I32, (sub, qb), 0)

    def transposed_chunks(x):
        xf = x.astype(F32)
        return jnp.concatenate([xf[:, c * LANES:(c + 1) * LANES].T for c in range(4)], axis=1).astype(BF16)

    wts = wi_ref[...].T[HEAD_DIM:HEAD_DIM + IDX_HEADS, :] * IDX_SCALE
    qit = transposed_chunks(qi_ref[...])
    for r0 in range(0, skv, mk):
        se = _dot(kiz_ref[r0:r0 + mk, 0:LANES], qit)
        so = _dot(kiz_ref[r0:r0 + mk, LANES:2 * LANES], qit)
        for s0 in range(0, mk, sub):
            acc = None
            for c4 in range(4):
                te = wts[2 * c4:2 * c4 + 1, :] * jnp.maximum(se[s0:s0 + sub, c4 * qb:(c4 + 1) * qb], 0.0)
                to = wts[2 * c4 + 1:2 * c4 + 2, :] * jnp.maximum(so[s0:s0 + sub, c4 * qb:(c4 + 1) * qb], 0.0)
                acc = te + to if acc is None else acc + (te + to)
            rows = slice(r0 + s0, r0 + s0 + sub)
            causal = (r0 + s0) + s_sub <= t_sub
            bits = pltpu.bitcast(jnp.where(causal, acc, -jnp.inf), I32)
            key = jnp.where(bits < 0, bits ^ 0x7FFFFFFF, bits)
            key_ref[rows, :] = key
            hi_ref[rows, :] = jnp.right_shift(key, 16).astype(jnp.int16)
            lo_ref[rows, :] = ((key & 0xFFFF) - HALF16).astype(jnp.int16)

    ge = lambda a, b: a >= b
    gt = lambda a, b: a > b
    lt = lambda a, b: a < b

    def search16(ref, need):
        def body(i, tu):
            cand = tu | jnp.left_shift(jnp.int32(1), 15 - i)
            cnt = _count16(ref, skv, (cand - HALF16).astype(jnp.int16), ge)
            return jnp.where(cnt >= need, cand, tu)
        return lax.fori_loop(0, 16, body, jnp.zeros((1, qb), I32))

    thi_u = search16(hi_ref, top)
    thi = (thi_u - HALF16).astype(jnp.int16)
    n_gt_hi = _count16(hi_ref, skv, thi, gt)
    thi_b = jnp.broadcast_to(thi, (PACK16, qb))
    for i in range(skv // PACK16):
        rs = slice(i * PACK16, (i + 1) * PACK16)
        lom_ref[rs, :] = jnp.where(hi_ref[rs, :] == thi_b, lo_ref[rs, :], jnp.int16(-HALF16))
    tlo_u = search16(lom_ref, top - n_gt_hi)
    thr = jnp.left_shift(thi_u - HALF16, 16) | tlo_u

    def count32(ref, cand, op):
        def body(c, acc):
            r0 = pl.multiple_of(c * ck, ck)
            m = op(ref[pl.ds(r0, ck), :], cand)
            return acc + jnp.sum(jnp.where(m, 1, 0).astype(I32).reshape(ck // SUBLANES, SUBLANES, qb), axis=0)
        acc = lax.fori_loop(0, nc, body, jnp.zeros((SUBLANES, qb), I32))
        return jnp.sum(acc, axis=0, keepdims=True)

    def bias_default(c, acc):
        r0 = pl.multiple_of(c * ck, ck)
        m = key_ref[pl.ds(r0, ck), :] >= thr
        sel = jnp.logical_and(m, r0 + s_iota <= t_idx)
        bias_ref[pl.ds(r0, ck), :] = jnp.where(sel, 0.0, -jnp.inf).astype(F32)
        return acc + jnp.sum(jnp.where(m, 1, 0).astype(I32).reshape(ck // SUBLANES, SUBLANES, qb), axis=0)

    n_ge = jnp.sum(lax.fori_loop(0, nc, bias_default, jnp.zeros((SUBLANES, qb), I32)), axis=0, keepdims=True)
    tie = jnp.logical_and(n_ge > top, thr != NEG_INF_KEY)
    any_tie = jnp.max(jnp.where(tie, 1, 0).astype(I32)) > 0

    @pl.when(any_tie)
    def _tie():
        nbits = (skv - 1).bit_length()
        need = top - count32(key_ref, thr, gt)
        big = jnp.int32(2 ** nbits)

        def fill(c, carry):
            r0 = pl.multiple_of(c * ck, ck)
            k = key_ref[pl.ds(r0, ck), :]
            tie_ref[pl.ds(r0, ck), :] = jnp.where(k == thr, r0 + s_iota, big)
            return carry

        lax.fori_loop(0, nc, fill, 0)

        def bit2(i, m):
            cand = m | jnp.left_shift(jnp.int32(1), nbits - 1 - i)
            cnt = count32(tie_ref, cand, lt)
            return jnp.where(cnt < need, cand, m)

        m = lax.fori_loop(0, nbits, bit2, jnp.zeros((1, qb), I32))

        def bias_tie(c, carry):
            r0 = pl.multiple_of(c * ck, ck)
            k = key_ref[pl.ds(r0, ck), :]
            sel = jnp.logical_or(k > thr, tie_ref[pl.ds(r0, ck), :] <= m)
            sel = jnp.logical_and(sel, r0 + s_iota <= t_idx)
            bias_ref[pl.ds(r0, ck), :] = jnp.where(sel, 0.0, -jnp.inf).astype(F32)
            return carry

        lax.fori_loop(0, nc, bias_tie, 0)

    qt = transposed_chunks(q_ref[...])
    o_parts = []
    for g in range(N_KV_HEADS):
        qet = qt[:, (2 * g) * qb:(2 * g + 2) * qb]
        m8 = jnp.full((SUBLANES, 4 * qb), -jnp.inf, F32)
        for r0 in range(0, skv, mk):
            le = _dot(kz_ref[r0:r0 + mk, (2 * g) * LANES:(2 * g + 1) * LANES], qet)
            lo = _dot(kz_ref[r0:r0 + mk, (2 * g + 1) * LANES:(2 * g + 2) * LANES], qet)
            for s0 in range(0, mk, sub):
                rows = slice(r0 + s0, r0 + s0 + sub)
                b = bias_ref[rows, :]
                l = jnp.concatenate([le[s0:s0 + sub] + jnp.concatenate([b, b], axis=1),
                                     lo[s0:s0 + sub] + jnp.concatenate([b, b], axis=1)], axis=1)
                l_ref[rows, :] = l
                m8 = jnp.maximum(m8, jnp.max(l.reshape(sub // SUBLANES, SUBLANES, 4 * qb), axis=0))
        mx = jnp.max(m8, axis=0, keepdims=True)

        def pbody(c, carry, mx=mx):
            r0 = pl.multiple_of(c * ck, ck)
            p_ref[pl.ds(r0, ck), :] = jnp.exp2(l_ref[pl.ds(r0, ck), :] - mx).astype(BF16)
            return carry

        lax.fori_loop(0, nc, pbody, 0)
        ot = _dot(vt_ref[g, :, 0:skv], p_ref[0:skv, :])
        o_parts.append(ot[0:HEAD_DIM, :] * (1.0 / ot[HEAD_DIM:HEAD_DIM + 1, :]))
    ot = jnp.concatenate(o_parts, axis=0)
    y = jnp.concatenate([ot[:, i * qb:(i + 1) * qb].T for i in range(4)], axis=1)
    o_ref[...] = y.astype(BF16)


def _attention(q, kz, v, qi, kiz, wi, seq, top):
    qb = LANES
    sq = pl.Squeezed()
    per_q = lambda c: pl.BlockSpec((sq, qb, c), lambda b, j: (b, j, 0))
    per_b = lambda c: pl.BlockSpec((sq, seq, c), lambda b, j: (b, 0, 0))
    return pl.pallas_call(
        functools.partial(_attn_kernel, seq, top),
        out_shape=jax.ShapeDtypeStruct((BATCH, seq, D_ATTN), BF16),
        grid=(BATCH, seq // qb),
        in_specs=[per_q(D_ATTN), per_b(4 * LANES), per_b(KV_DIM), per_q(512), per_b(2 * LANES), per_q(LANES)],
        out_specs=per_q(D_ATTN),
        scratch_shapes=[
            pltpu.VMEM((N_KV_HEADS, HEAD_DIM + PACK16, seq), BF16),
            pltpu.VMEM((seq, qb), I32),
            pltpu.VMEM((seq, qb), jnp.int16),
            pltpu.VMEM((seq, qb), jnp.int16),
            pltpu.VMEM((seq, qb), jnp.int16),
            pltpu.VMEM((seq, qb), I32),
            pltpu.VMEM((seq, qb), F32),
            pltpu.VMEM((seq, 4 * qb), F32),
            pltpu.VMEM((seq, 4 * qb), BF16),
        ],
        compiler_params=pltpu.CompilerParams(dimension_semantics=("parallel", "arbitrary"),
                                             vmem_limit_bytes=VMEM_LIMIT),
        name="dsa_attention",
    )(q, kz, v, qi, kiz, wi)


def _outproj_kernel(ys_ref, ya_ref, h_ref, w_ref, g_ref, b_ref, o_ref):
    mix = _dot(ys_ref[...], w_ref[0:2 * D_S5, :]) + _dot(ya_ref[...], w_ref[2 * D_S5:, :])
    o_ref[...] = _ln(ALPHA * h_ref[...] + mix, g_ref[...], b_ref[...])


def _outproj(y_scan, y_at, h, w_out_p, g, b):
    t = h.shape[0]
    tm = min(t, 512)
    row = lambda c: pl.BlockSpec((tm, c), lambda i: (i, 0))
    return pl.pallas_call(
        _outproj_kernel,
        out_shape=jax.ShapeDtypeStruct((t, D_MODEL), F32),
        grid=(t // tm,),
        in_specs=[row(2 * D_S5), row(D_ATTN), row(D_MODEL), _full_spec((D_MIX, D_MODEL)),
                  _full_spec((1, D_MODEL)), _full_spec((1, D_MODEL))],
        out_specs=row(D_MODEL),
        compiler_params=pltpu.CompilerParams(dimension_semantics=("parallel",),
                                             vmem_limit_bytes=VMEM_LIMIT),
        name="outproj_ln",
    )(y_scan, y_at, h, w_out_p, g, b)


def _ffn_kernel(h_ref, p_ref, wg_ref, wu_ref, wd_ref, wpg_ref, wpp_ref, g_ref, b_ref, o_ref):
    h = h_ref[...]
    hb = h.astype(BF16)
    gate = _dot(hb, wg_ref[...])
    up = _dot(hb, wu_ref[...])
    act = (gate * _sigmoid(gate) * up).astype(BF16)
    ffn = _dot(act, wd_ref[...])
    ple = _sigmoid(_dot(hb, wpg_ref[...])) * _dot(p_ref[...].astype(BF16), wpp_ref[...])
    o_ref[...] = _ln(ALPHA * h + ffn + ple, g_ref[...], b_ref[...])


def _ffn(h, p, wg, wu, wd, wpg, wpp, g, b):
    t = h.shape[0]
    tm = min(t, 512)
    row = lambda c: pl.BlockSpec((tm, c), lambda i: (i, 0))
    const = lambda shape: pl.BlockSpec(shape, lambda i: (0, 0), pipeline_mode=pl.Buffered(1))
    return pl.pallas_call(
        _ffn_kernel,
        out_shape=jax.ShapeDtypeStruct((t, D_MODEL), F32),
        grid=(t // tm,),
        in_specs=[row(D_MODEL), row(D_PLE), const((D_MODEL, D_FF)), const((D_MODEL, D_FF)),
                  const((D_FF, D_MODEL)), const((D_MODEL, D_MODEL)), const((D_PLE, D_MODEL)),
                  _full_spec((1, D_MODEL)), _full_spec((1, D_MODEL))],
        out_specs=row(D_MODEL),
        compiler_params=pltpu.CompilerParams(dimension_semantics=("parallel",),
                                             vmem_limit_bytes=VMEM_LIMIT),
        name="ffn_ple_ln",
    )(h, p, wg, wu, wd, wpg, wpp, g, b)


def _block_diag(w):
    hh, n, m = w.shape
    eye = jnp.eye(hh, dtype=w.dtype)
    return jnp.einsum('hij,hk->hikj', w, eye).reshape(hh * n, hh * m)


def _s5_out_matrix(c_re, c_im):
    cre = _block_diag(c_re.transpose(0, 2, 1))
    cim = _block_diag(c_im.transpose(0, 2, 1))
    return jnp.concatenate([cre, -cim], axis=0).astype(BF16)


def _attn_out_rows(w_at):
    w = w_at.reshape(N_KV_HEADS, 4, HEAD_DIM, D_MODEL)
    w = w[:, jnp.array([0, 2, 1, 3])]
    return w.transpose(1, 0, 2, 3).reshape(D_ATTN, D_MODEL)


def _forward(seq, top, x, p, positions, ln_emb_g, ln_emb_b, w_in,
             s5_lam_re, s5_lam_im, s5_log_step, s5_b_re, s5_b_im, s5_c_re, s5_c_im,
             s5_d, s5_w_glu, s5_b_glu,
             rg_conv_w, rg_conv_b, rg_wa, rg_ba, rg_wx, rg_bx, rg_lam,
             w_out, ln1_g, ln1_b, ffn_w_up, ffn_w_down, ple_w_gate, ple_w_proj,
             ln2_g, ln2_b):
    t = BATCH * seq
    cos, sin = _rope_tables(positions, seq)
    row = lambda a: a.reshape(1, -1)
    h3 = x
    for i in range(DEPTH):
        w_in_p = jnp.pad(w_in[i], ((0, 0), (0, N_IN_PAD - N_IN))).astype(BF16)
        outs = _inproj(h3, w_in_p, cos, sin, row(ln_emb_g), row(ln_emb_b), i == 0, seq)
        if i == 0:
            h3, outs = outs[0], outs[1:]
        scan_in, q, kz, v, qi, kiz, wi = outs
        a8, bd = _s5_params(s5_lam_re[i], s5_lam_im[i], s5_log_step[i], s5_b_re[i], s5_b_im[i])
        y_scan = _scans(scan_in, a8, bd, _s5_out_matrix(s5_c_re[i], s5_c_im[i]), row(s5_d[i]),
                        s5_w_glu[i].astype(BF16), row(s5_b_glu[i]),
                        rg_conv_w[i], row(rg_conv_b[i]),
                        _block_diag(rg_wa[i]).astype(BF16), row(rg_ba[i]),
                        _block_diag(rg_wx[i]).astype(BF16), row(rg_bx[i]), row(rg_lam[i]), seq)
        y_at = _attention(q, kz, v, qi, kiz, wi, seq, top)
        w_out_p = jnp.concatenate([w_out[i][:2 * D_S5], _attn_out_rows(w_out[i][2 * D_S5:])], axis=0).astype(BF16)
        h1 = _outproj(y_scan.reshape(t, 2 * D_S5), y_at.reshape(t, D_ATTN), h3.reshape(t, D_MODEL),
                      w_out_p, row(ln1_g[i]), row(ln1_b[i]))
        h2 = _ffn(h1, p[i].reshape(t, D_PLE),
                  ffn_w_up[i][:, :D_FF].astype(BF16), ffn_w_up[i][:, D_FF:].astype(BF16),
                  ffn_w_down[i].astype(BF16), ple_w_gate[i].astype(BF16), ple_w_proj[i].astype(BF16),
                  row(ln2_g[i]), row(ln2_b[i]))
        h3 = h2.reshape(BATCH, seq, D_MODEL)
    return h3


def kernel(x, p, positions, ln_emb_g, ln_emb_b, w_in, s5_lam_re, s5_lam_im, s5_log_step, s5_b_re, s5_b_im, s5_c_re, s5_c_im, s5_d, s5_w_glu, s5_b_glu, rg_conv_w, rg_conv_b, rg_wa, rg_ba, rg_wx, rg_bx, rg_lam, w_out, ln1_g, ln1_b, ffn_w_up, ffn_w_down, ple_w_gate, ple_w_proj, ln2_g, ln2_b):
    seq = x.shape[1]
    return _forward(seq, min(TOPK_MAX, seq // 4), x, p, positions, ln_emb_g, ln_emb_b, w_in,
                    s5_lam_re, s5_lam_im, s5_log_step, s5_b_re, s5_b_im, s5_c_re, s5_c_im,
                    s5_d, s5_w_glu, s5_b_glu,
                    rg_conv_w, rg_conv_b, rg_wa, rg_ba, rg_wx, rg_bx, rg_lam,
                    w_out, ln1_g, ln1_b, ffn_w_up, ffn_w_down, ple_w_gate, ple_w_proj,
                    ln2_g, ln2_b)
```

```python
import functools
import math

import jax
import jax.numpy as jnp
from jax import lax
from jax.experimental import pallas as pl
from jax.experimental.pallas import tpu as pltpu

F32 = jnp.float32
BF16 = jnp.bfloat16
I32 = jnp.int32

D_MODEL = 1024
BATCH = 8
SEQ = 2048
DEPTH = 2
D_S5 = 256
S5_GROUP = 16
S5_GROUPS = 16
S5_STATE = 64
N_S5_STATE = S5_GROUPS * S5_STATE
D_RG = 256
RG_BLOCKS = 8
RG_BLOCK = 32
RG_CONV = 4
RG_C = 8.0
N_HEADS = 8
N_KV_HEADS = 2
HEAD_DIM = 64
D_ATTN = 512
KV_DIM = 128
IDX_HEADS = 8
IDX_DIM = 64
TOPK_MAX = 256
D_MIX = 1024
ROPE_THETA = 10000.0
D_FF = 2816
D_PLE = 256
ALPHA = (2.0 * DEPTH) ** 0.25
LN_EPS = 1e-5
N_IN = 2120
N_IN_PAD = 2176
ATT_SCALE = HEAD_DIM ** -0.5
IDX_SCALE = (IDX_HEADS * IDX_DIM) ** -0.5
LOG2E = math.log2(math.e)

LANES = 128
SUBLANES = 8
PACK16 = 16
HALF16 = 32768
KV_CLASS = 256
ATT_CHUNK = 256
ATT_MM_CHUNK = 512
ATT_SUB = 64
NEG_INF_CODE = 127

VMEM_LIMIT = 56 * 1024 * 1024


def _dot(a, b):
    return jnp.dot(a, b, preferred_element_type=F32)


def _dot_nt(a, b):
    return lax.dot_general(a, b, (((1,), (1,)), ((), ())), preferred_element_type=F32)


def _ln(x, g, b):
    mu = jnp.mean(x, axis=-1, keepdims=True)
    xc = x - mu
    var = jnp.mean(xc * xc, axis=-1, keepdims=True)
    return xc * lax.rsqrt(var + LN_EPS) * g + b


def _gelu(x):
    c = math.sqrt(2.0 / math.pi)
    return 0.5 * x * (1.0 + jnp.tanh(c * (x + 0.044715 * (x * x * x))))


def _sigmoid(x):
    return 1.0 / (1.0 + jnp.exp(-x))


def _full_spec(shape):
    n = len(shape)
    return pl.BlockSpec(shape, lambda *_: (0,) * n)


def _rope_table_kernel(pos_ref, inv_ref, sgn_ref, cos_ref, sin_ref):
    ang = pos_ref[...].astype(F32) * inv_ref[...]
    cos_ref[...] = jnp.cos(ang)
    sin_ref[...] = jnp.sin(ang) * sgn_ref[...]


def _rope_tables(positions, seq):
    t = BATCH * seq
    tm = min(t, 2048)
    inv = ROPE_THETA ** (-jnp.arange(0, HEAD_DIM, 2, dtype=F32) / HEAD_DIM)
    inv128 = jnp.tile(inv, 4)[None, :]
    sgn = jnp.where((jnp.arange(LANES) % HEAD_DIM) < HEAD_DIM // 2, -1.0, 1.0).astype(F32)[None, :]
    pos = positions.reshape(t, 1)
    cos, sin = pl.pallas_call(
        _rope_table_kernel,
        out_shape=(jax.ShapeDtypeStruct((t, LANES), F32),) * 2,
        grid=(t // tm,),
        in_specs=[pl.BlockSpec((tm, 1), lambda i: (i, 0)), _full_spec((1, LANES)), _full_spec((1, LANES))],
        out_specs=(pl.BlockSpec((tm, LANES), lambda i: (i, 0)),) * 2,
        compiler_params=pltpu.CompilerParams(dimension_semantics=("parallel",)),
        name="rope_tables",
    )(pos, inv128, sgn)
    return cos.reshape(BATCH, seq, LANES), sin.reshape(BATCH, seq, LANES)


def _inproj_kernel(apply_ln, tt, h_ref, w_ref, cos_ref, sin_ref, g_ref, b_ref, *outs):
    if apply_ln:
        hn_ref, so_ref, q_ref, kz_ref, v_ref, qi_ref, kiz_ref, wi_ref = outs
    else:
        so_ref, q_ref, kz_ref, v_ref, qi_ref, kiz_ref, wi_ref = outs
    rows = BATCH * tt
    h = h_ref[...].reshape(rows, D_MODEL)
    if apply_ln:
        h = _ln(h, g_ref[...], b_ref[...])
        hn_ref[...] = h.reshape(BATCH, tt, D_MODEL)
    hb = h.astype(BF16)
    cos = cos_ref[...].reshape(rows, LANES)
    sin = sin_ref[...].reshape(rows, LANES)
    lane = lax.broadcasted_iota(I32, (rows, LANES), 1)
    first_half = (lane & (HEAD_DIM // 2)) == 0
    lo64 = lane < HEAD_DIM

    def rope(x, cs, sn):
        partner = jnp.where(first_half, pltpu.roll(x, LANES - 32, 1), pltpu.roll(x, 32, 1))
        return x * cs + partner * sn

    def rope_wide(x, n_chunks):
        return jnp.concatenate(
            [rope(x[:, c * LANES:(c + 1) * LANES], cos, sin) for c in range(n_chunks)], axis=1)

    ps = _dot(hb, w_ref[:, 0:768])
    for j in range(6):
        for b in range(BATCH):
            so_ref[j, pl.ds(b, tt, stride=BATCH), :] = ps[b * tt:(b + 1) * tt, j * LANES:(j + 1) * LANES]

    pq = _dot(hb, w_ref[:, 768:1280])
    q_ref[...] = (rope_wide(pq, 4) * (ATT_SCALE * LOG2E)).astype(BF16).reshape(BATCH, tt, D_ATTN)

    pkv = _dot(hb, w_ref[:, 1280:1536])
    kr = rope(pkv[:, 0:LANES], cos, sin)
    ksw = pltpu.roll(kr, HEAD_DIM, 1)
    zero = jnp.zeros_like(kr)
    kz = jnp.concatenate([jnp.where(lo64, kr, zero), jnp.where(lo64, zero, ksw),
                          jnp.where(lo64, ksw, zero), jnp.where(lo64, zero, kr)], axis=1)
    kz_ref[...] = kz.astype(BF16).reshape(BATCH, tt, 4 * LANES)
    v_ref[...] = pkv[:, LANES:2 * LANES].astype(BF16).reshape(BATCH, tt, KV_DIM)

    pqi = _dot(hb, w_ref[:, 1536:2048])
    qi_ref[...] = rope_wide(pqi, 4).astype(BF16).reshape(BATCH, tt, IDX_HEADS * IDX_DIM)

    pk = _dot(hb, w_ref[:, 2048:N_IN_PAD])
    kir = rope(pk, jnp.where(lo64, cos, 1.0), jnp.where(lo64, sin, 0.0))
    kie = jnp.where(lo64, kir, zero)
    kiz = jnp.concatenate([kie, pltpu.roll(kie, HEAD_DIM, 1)], axis=1)
    kiz_ref[...] = kiz.astype(BF16).reshape(BATCH, tt, 2 * LANES)
    wi_ref[...] = pk.reshape(BATCH, tt, LANES)


def _inproj(h3, w_in_p, cos, sin, ln_g, ln_b, apply_ln, seq):
    tt = min(seq, 128)
    nt = seq // tt
    blk = lambda c: pl.BlockSpec((BATCH, tt, c), lambda i: (0, i, 0))
    out_shape = [
        jax.ShapeDtypeStruct((6, seq * BATCH, LANES), F32),
        jax.ShapeDtypeStruct((BATCH, seq, D_ATTN), BF16),
        jax.ShapeDtypeStruct((BATCH, seq, 4 * LANES), BF16),
        jax.ShapeDtypeStruct((BATCH, seq, KV_DIM), BF16),
        jax.ShapeDtypeStruct((BATCH, seq, IDX_HEADS * IDX_DIM), BF16),
        jax.ShapeDtypeStruct((BATCH, seq, 2 * LANES), BF16),
        jax.ShapeDtypeStruct((BATCH, seq, LANES), F32),
    ]
    out_specs = [pl.BlockSpec((6, tt * BATCH, LANES), lambda i: (0, i, 0)),
                 blk(D_ATTN), blk(4 * LANES), blk(KV_DIM), blk(512), blk(2 * LANES), blk(LANES)]
    if apply_ln:
        out_shape = [jax.ShapeDtypeStruct((BATCH, seq, D_MODEL), F32)] + out_shape
        out_specs = [blk(D_MODEL)] + out_specs
    return pl.pallas_call(
        functools.partial(_inproj_kernel, apply_ln, tt),
        out_shape=tuple(out_shape),
        grid=(nt,),
        in_specs=[blk(D_MODEL), _full_spec((D_MODEL, N_IN_PAD)), blk(LANES), blk(LANES),
                  _full_spec((1, D_MODEL)), _full_spec((1, D_MODEL))],
        out_specs=tuple(out_specs),
        compiler_params=pltpu.CompilerParams(dimension_semantics=("parallel",),
                                             vmem_limit_bytes=VMEM_LIMIT),
        name="inproj_ln" if apply_ln else "inproj",
    )(h3, w_in_p, cos, sin, ln_g, ln_b)


def _s5_param_kernel(lr_ref, li_ref, ls_ref, br_ref, bi_ref, a_ref, bd_ref):
    lr = lr_ref[...]
    li = li_ref[...]
    step = jnp.exp(ls_ref[...])
    mag = jnp.exp(lr * step)
    ar = mag * jnp.cos(li * step)
    ai = mag * jnp.sin(li * step)
    den = lr * lr + li * li
    nr, ni = ar - 1.0, ai
    cr = (nr * lr + ni * li) / den
    ci = (ni * lr - nr * li) / den
    br = br_ref[...]
    bi = bi_ref[...]
    bbr = cr * br - ci * bi
    bbi = cr * bi + ci * br
    row = lax.broadcasted_iota(I32, (D_S5, N_S5_STATE), 0)
    col = lax.broadcasted_iota(I32, (D_S5, N_S5_STATE), 1)
    blk = (row // S5_GROUP) == (col // S5_STATE)
    zero = jnp.zeros((D_S5, N_S5_STATE), F32)
    bd_ref[:, 0:N_S5_STATE] = jnp.where(blk, jnp.concatenate([bbr] * S5_GROUPS, axis=0), zero).astype(BF16)
    bd_ref[:, N_S5_STATE:] = jnp.where(blk, jnp.concatenate([bbi] * S5_GROUPS, axis=0), zero).astype(BF16)
    a_ref[:, 0:N_S5_STATE] = jnp.broadcast_to(ar, (SUBLANES, N_S5_STATE))
    a_ref[:, N_S5_STATE:] = jnp.broadcast_to(ai, (SUBLANES, N_S5_STATE))


def _s5_params(lam_re, lam_im, log_step, b_re, b_im):
    lr = lam_re.reshape(1, N_S5_STATE)
    li = lam_im.reshape(1, N_S5_STATE)
    ls = jnp.repeat(log_step, S5_STATE).reshape(1, N_S5_STATE)
    br = b_re.transpose(2, 0, 1).reshape(S5_GROUP, N_S5_STATE)
    bi = b_im.transpose(2, 0, 1).reshape(S5_GROUP, N_S5_STATE)
    return pl.pallas_call(
        _s5_param_kernel,
        out_shape=(jax.ShapeDtypeStruct((SUBLANES, 2 * N_S5_STATE), F32),
                   jax.ShapeDtypeStruct((D_S5, 2 * N_S5_STATE), BF16)),
        name="s5_params",
    )(lr, li, ls, br, bi)


def _scan_kernel(tt, si_ref, a_ref, bd_ref, cd_ref, dsk_ref, wglu_ref, bglu_ref,
                 cw_ref, cb_ref, wa_ref, ba_ref, wx_ref, bx_ref, lam_ref,
                 o_ref, x_ref, hs_ref, halo_ref, ab_ref, hrg_ref, y_ref):
    rows = BATCH * tt
    i = pl.program_id(0)

    @pl.when(i == 0)
    def _init():
        hs_ref[...] = jnp.zeros_like(hs_ref)
        halo_ref[...] = jnp.zeros_like(halo_ref)
        hrg_ref[...] = jnp.zeros_like(hrg_ref)

    u = jnp.concatenate([si_ref[0], si_ref[1]], axis=1)
    x_ref[...] = _dot(u.astype(BF16), bd_ref[...])
    half = N_S5_STATE // 2
    for hh in range(2):
        lo = hh * half
        ar = a_ref[:, lo:lo + half]
        ai = a_ref[:, N_S5_STATE + lo:N_S5_STATE + lo + half]

        def step(t, carry, lo=lo, ar=ar, ai=ai):
            hr, hi = carry
            r0 = pl.multiple_of(t * BATCH, BATCH)
            br = x_ref[pl.ds(r0, BATCH), lo:lo + half]
            bi = x_ref[pl.ds(r0, BATCH), N_S5_STATE + lo:N_S5_STATE + lo + half]
            nr = ar * hr - ai * hi + br
            ni = ar * hi + ai * hr + bi
            x_ref[pl.ds(r0, BATCH), lo:lo + half] = nr
            x_ref[pl.ds(r0, BATCH), N_S5_STATE + lo:N_S5_STATE + lo + half] = ni
            return nr, ni

        hr, hi = lax.fori_loop(
            0, tt, step,
            (hs_ref[:, lo:lo + half], hs_ref[:, N_S5_STATE + lo:N_S5_STATE + lo + half]),
            unroll=True)
        hs_ref[:, lo:lo + half] = hr
        hs_ref[:, N_S5_STATE + lo:N_S5_STATE + lo + half] = hi

    y = _dot(x_ref[...].astype(BF16), cd_ref[...]) + dsk_ref[...] * u
    y = _gelu(y)
    y = y * _sigmoid(_dot(y.astype(BF16), wglu_ref[...]) + bglu_ref[...])
    y_ref[0] = y[:, 0:LANES]
    y_ref[1] = y[:, LANES:2 * LANES]

    xr = jnp.concatenate([si_ref[2], si_ref[3]], axis=1)
    gate = jnp.concatenate([si_ref[4], si_ref[5]], axis=1)
    hal = (RG_CONV - 1) * BATCH
    xext = jnp.concatenate([halo_ref[...], xr], axis=0)
    halo_ref[...] = xr[rows - hal:rows, :]
    xc = cb_ref[...]
    for k in range(RG_CONV):
        xc = xc + cw_ref[k:k + 1, :] * xext[k * BATCH:k * BATCH + rows, :]
    xcb = xc.astype(BF16)
    r = _sigmoid(_dot(xcb, wa_ref[...]) + ba_ref[...])
    ig = _sigmoid(_dot(xcb, wx_ref[...]) + bx_ref[...])
    nl = -lam_ref[...]
    softplus = jnp.maximum(nl, 0.0) + jnp.log(1.0 + jnp.exp(-jnp.abs(nl)))
    log_a = -RG_C * r * softplus
    a = jnp.exp(log_a)
    mult = jnp.sqrt(1.0 - a * a)
    ab_ref[0] = a
    ab_ref[1] = mult * (ig * xc)

    def rg_step(t, h):
        r0 = pl.multiple_of(t * BATCH, BATCH)
        hn = ab_ref[0, pl.ds(r0, BATCH), :] * h + ab_ref[1, pl.ds(r0, BATCH), :]
        ab_ref[1, pl.ds(r0, BATCH), :] = hn
        return hn

    hrg_ref[...] = lax.fori_loop(0, tt, rg_step, hrg_ref[...], unroll=True)
    yr = ab_ref[1] * _gelu(gate)
    y_ref[2] = yr[:, 0:LANES]
    y_ref[3] = yr[:, LANES:2 * LANES]

    for b in range(BATCH):
        o_ref[b] = jnp.concatenate(
            [y_ref[j, pl.ds(b, tt, stride=BATCH), :] for j in range(4)], axis=1).astype(BF16)


def _scans(scan_in, a8, bd, cd, dsk, wglu, bglu, cw, cb, wa, ba, wx, bx, lam, seq):
    tt = min(seq, 64)
    nt = seq // tt
    rows = tt * BATCH
    params = [a8, bd, cd, dsk, wglu, bglu, cw, cb, wa, ba, wx, bx, lam]
    return pl.pallas_call(
        functools.partial(_scan_kernel, tt),
        out_shape=jax.ShapeDtypeStruct((BATCH, seq, 2 * D_S5), BF16),
        grid=(nt,),
        in_specs=[pl.BlockSpec((6, rows, LANES), lambda i: (0, i, 0))] + [_full_spec(p.shape) for p in params],
        out_specs=pl.BlockSpec((BATCH, tt, 2 * D_S5), lambda i: (0, i, 0)),
        scratch_shapes=[
            pltpu.VMEM((rows, 2 * N_S5_STATE), F32),
            pltpu.VMEM((BATCH, 2 * N_S5_STATE), F32),
            pltpu.VMEM(((RG_CONV - 1) * BATCH, D_RG), F32),
            pltpu.VMEM((2, rows, D_RG), F32),
            pltpu.VMEM((BATCH, D_RG), F32),
            pltpu.VMEM((4, rows, LANES), F32),
        ],
        compiler_params=pltpu.CompilerParams(dimension_semantics=("arbitrary",),
                                             vmem_limit_bytes=VMEM_LIMIT),
        name="scans",
    )(scan_in, *params)


def _count16(ref, skv, cand, op):
    cb = jnp.broadcast_to(cand, (PACK16, LANES))
    accs = []
    for i in range(skv // PACK16):
        one = jnp.where(op(ref[i * PACK16:(i + 1) * PACK16, :], cb), jnp.int16(1), jnp.int16(0))
        if i < 4:
            accs.append(one)
        else:
            accs[i % 4] = accs[i % 4] + one
    tot = (accs[0] + accs[1]) + (accs[2] + accs[3])
    return jnp.sum(tot.astype(I32), axis=0, keepdims=True)


def _attn_kernel(seq, top, q_ref, kz_ref, v_ref, qi_ref, kiz_ref, wi_ref, o_ref,
                 vt_ref, sc_ref, sb_ref, lom_ref, tie_ref, bias_ref, l_ref, p_ref):
    j = pl.program_id(1)

    @pl.when(j == 0)
    def _():
        vt = v_ref[...].astype(F32).T.astype(BF16)
        ones = jnp.where(lax.broadcasted_iota(I32, (PACK16, seq), 0) == 0, 1.0, 0.0).astype(BF16)
        for g in range(N_KV_HEADS):
            vt_ref[g, 0:HEAD_DIM, :] = vt[g * HEAD_DIM:(g + 1) * HEAD_DIM, :]
            vt_ref[g, HEAD_DIM:HEAD_DIM + PACK16, :] = ones

    cls_rows = min(seq, KV_CLASS)
    for c in range(seq // cls_rows):
        @pl.when(j // (cls_rows // LANES) == c)
        def _(c=c):
            _attn_class(cls_rows * (c + 1), top, j, q_ref, kz_ref, qi_ref, kiz_ref, wi_ref, o_ref,
                        vt_ref, sc_ref, sb_ref, lom_ref, tie_ref, bias_ref, l_ref, p_ref)


def _attn_class(skv, top, j, q_ref, kz_ref, qi_ref, kiz_ref, wi_ref, o_ref,
                vt_ref, sc_ref, sb_ref, lom_ref, tie_ref, bias_ref, l_ref, p_ref):
    ck = ATT_CHUNK
    nc = skv // ck
    mm_chunks = [(r0, min(ATT_MM_CHUNK, skv - r0)) for r0 in range(0, skv, ATT_MM_CHUNK)]
    sub = ATT_SUB
    qb = LANES
    t_idx = j * qb + lax.broadcasted_iota(I32, (ck, qb), 1)
    s_iota = lax.broadcasted_iota(I32, (ck, qb), 0)
    t_sub = j * qb + lax.broadcasted_iota(I32, (sub, qb), 1)
    s_sub = lax.broadcasted_iota(I32, (sub, qb), 0)

    def transposed_chunks(x):
        xf = x.astype(F32)
        return jnp.concatenate([xf[:, c * LANES:(c + 1) * LANES].T for c in range(4)], axis=1).astype(BF16)

    wts = wi_ref[...].T[HEAD_DIM:HEAD_DIM + IDX_HEADS, :] * IDX_SCALE
    qit = transposed_chunks(qi_ref[...])
    for r0, mk in mm_chunks:
        se = _dot(kiz_ref[r0:r0 + mk, 0:LANES], qit)
        so = _dot(kiz_ref[r0:r0 + mk, LANES:2 * LANES], qit)
        for s0 in range(0, mk, sub):
            acc = None
            for c4 in range(4):
                te = wts[2 * c4:2 * c4 + 1, :] * jnp.maximum(se[s0:s0 + sub, c4 * qb:(c4 + 1) * qb], 0.0)
                to = wts[2 * c4 + 1:2 * c4 + 2, :] * jnp.maximum(so[s0:s0 + sub, c4 * qb:(c4 + 1) * qb], 0.0)
                acc = te + to if acc is None else acc + (te + to)
            rows = slice(r0 + s0, r0 + s0 + sub)
            causal = (r0 + s0) + s_sub <= t_sub
            sc = jnp.where(causal, acc, -jnp.inf)
            sc_ref[rows, :] = sc
            near = sc.astype(BF16)
            rounded_up = (sc - near.astype(F32)).astype(BF16) < 0
            nb = pltpu.bitcast(near, jnp.int16)
            below = pltpu.bitcast(nb + jnp.where(nb < 0, jnp.int16(1), jnp.int16(-1)), BF16)
            sb_ref[rows, :] = jnp.where(rounded_up, below, near)

    ge = lambda a, b: a >= b
    lt = lambda a, b: a < b

    def f32_of_code(u):
        k = u - HALF16
        return pltpu.bitcast(jnp.left_shift(jnp.where(k < 0, k ^ 0x7FFF, k), 16), F32)

    def coarse_body(i, tu):
        cand = jnp.minimum(tu | jnp.left_shift(jnp.int32(1), 15 - i), 65535 - NEG_INF_CODE)
        cnt = _count16(sb_ref, skv, f32_of_code(cand + NEG_INF_CODE).astype(BF16), ge)
        return jnp.where(cnt >= top, cand, tu)

    lo_code = lax.fori_loop(0, 16, coarse_body, jnp.zeros((1, qb), I32)) + NEG_INF_CODE
    lo_f = f32_of_code(lo_code)
    hi_f = f32_of_code(lo_code + 1)
    lo_b = lo_f.astype(BF16)
    hi_b = hi_f.astype(BF16)
    keep, drop = jnp.zeros((), BF16), jnp.full((), -jnp.inf, BF16)
    n_above = _count16(sb_ref, skv, hi_b, ge)
    width = hi_f - lo_f
    usable = jnp.logical_and(width > 1e-30, width < 1e30)
    scale = jnp.where(usable, 65536.0 / jnp.where(usable, width, 1.0), 0.0)
    outside = jnp.int16(-HALF16)

    def residuals(c, carry):
        r0 = pl.multiple_of(c * ck, ck)
        s = sc_ref[pl.ds(r0, ck), :]
        x = jnp.clip(jnp.where(s > lo_f, s - lo_f, 0.0) * scale, 0.0, 65534.0)
        rq = (x.astype(I32) - (HALF16 - 1)).astype(jnp.int16)
        sb = sb_ref[pl.ds(r0, ck), :]
        inside = jnp.logical_and(sb >= lo_b, sb < hi_b)
        lom_ref[pl.ds(r0, ck), :] = jnp.where(inside, rq, outside)
        return carry

    lax.fori_loop(0, nc, residuals, 0)

    def fine_body(i, tu):
        cand = tu | jnp.left_shift(jnp.int32(1), 15 - i)
        cnt = _count16(lom_ref, skv, (cand - HALF16).astype(jnp.int16), ge)
        return jnp.where(cnt >= top - n_above, cand, tu)

    cut = (lax.fori_loop(0, 16, fine_body, jnp.zeros((1, qb), I32)) - HALF16).astype(jnp.int16)

    def bias_default(c, acc):
        r0 = pl.multiple_of(c * ck, ck)
        at_cut = lom_ref[pl.ds(r0, ck), :] >= cut
        sel = jnp.logical_or(sb_ref[pl.ds(r0, ck), :] >= hi_b, at_cut)
        bias_ref[pl.ds(r0, ck), :] = jnp.where(sel, keep, drop)
        ones = jnp.where(at_cut, jnp.int16(1), jnp.int16(0))
        parts = [ones[i * PACK16:(i + 1) * PACK16, :] for i in range(ck // PACK16)]
        while len(parts) > 1:
            parts = [a + b for a, b in zip(parts[0::2], parts[1::2])]
        return acc + parts[0]

    n_cut = lax.fori_loop(0, nc, bias_default, jnp.zeros((PACK16, qb), jnp.int16))
    n_ge = n_above + jnp.sum(n_cut.astype(I32), axis=0, keepdims=True)
    tie = jnp.logical_and(n_ge > top, lo_code != NEG_INF_CODE)
    any_tie = jnp.max(jnp.where(tie, 1, 0).astype(I32)) > 0

    @pl.when(any_tie)
    def _tie():
        nbits = (skv - 1).bit_length()
        n_over = _count16(lom_ref, skv, cut, lambda a, b: a > b)
        need = top - n_above - n_over
        big = jnp.int16(2 ** nbits)

        def fill(c, carry):
            r0 = pl.multiple_of(c * ck, ck)
            idx = (r0 + s_iota).astype(jnp.int16)
            tie_ref[pl.ds(r0, ck), :] = jnp.where(lom_ref[pl.ds(r0, ck), :] == cut, idx, big)
            return carry

        lax.fori_loop(0, nc, fill, 0)

        def bit2(i, m):
            cand = m | jnp.left_shift(jnp.int32(1), nbits - 1 - i)
            cnt = _count16(tie_ref, skv, cand.astype(jnp.int16), lt)
            return jnp.where(cnt < need, cand, m)

        m = lax.fori_loop(0, nbits, bit2, jnp.zeros((1, qb), I32)).astype(jnp.int16)

        def bias_tie(c, carry):
            r0 = pl.multiple_of(c * ck, ck)
            sel = jnp.logical_or(sb_ref[pl.ds(r0, ck), :] >= hi_b,
                                 lom_ref[pl.ds(r0, ck), :] > cut)
            sel = jnp.logical_or(sel, tie_ref[pl.ds(r0, ck), :] <= m)
            bias_ref[pl.ds(r0, ck), :] = jnp.where(sel, keep, drop)
            return carry

        lax.fori_loop(0, nc, bias_tie, 0)

    qt = transposed_chunks(q_ref[...])
    o_parts = []
    for g in range(N_KV_HEADS):
        qet = qt[:, (2 * g) * qb:(2 * g + 2) * qb]
        m8 = jnp.full((SUBLANES, 4 * qb), -jnp.inf, F32)
        for r0, mk in mm_chunks:
            le = _dot(kz_ref[r0:r0 + mk, (2 * g) * LANES:(2 * g + 1) * LANES], qet)
            lo = _dot(kz_ref[r0:r0 + mk, (2 * g + 1) * LANES:(2 * g + 2) * LANES], qet)
            for s0 in range(0, mk, sub):
                rows = slice(r0 + s0, r0 + s0 + sub)
                b = bias_ref[rows, :].astype(F32)
                if r0 + s0 + sub > skv - KV_CLASS:
                    b = jnp.where((r0 + s0) + s_sub <= t_sub, b, -jnp.inf)
                l = jnp.concatenate([le[s0:s0 + sub] + jnp.concatenate([b, b], axis=1),
                                     lo[s0:s0 + sub] + jnp.concatenate([b, b], axis=1)], axis=1)
                l_ref[rows, :] = l
                m8 = jnp.maximum(m8, jnp.max(l.reshape(sub // SUBLANES, SUBLANES, 4 * qb), axis=0))
        mx = jnp.max(m8, axis=0, keepdims=True)

        def pbody(c, carry, mx=mx):
            r0 = pl.multiple_of(c * ck, ck)
            p_ref[pl.ds(r0, ck), :] = jnp.exp2(l_ref[pl.ds(r0, ck), :] - mx).astype(BF16)
            return carry

        lax.fori_loop(0, nc, pbody, 0)
        ot = _dot(vt_ref[g, :, 0:skv], p_ref[0:skv, :])
        o_parts.append(ot[0:HEAD_DIM, :] * (1.0 / ot[HEAD_DIM:HEAD_DIM + 1, :]))
    ot = jnp.concatenate(o_parts, axis=0)
    y = jnp.concatenate([ot[:, i * qb:(i + 1) * qb].T for i in range(4)], axis=1)
    o_ref[...] = y.astype(BF16)


def _attention(q, kz, v, qi, kiz, wi, seq, top):
    qb = LANES
    sq = pl.Squeezed()
    per_q = lambda c: pl.BlockSpec((sq, qb, c), lambda b, j: (b, j, 0))
    per_b = lambda c: pl.BlockSpec((sq, seq, c), lambda b, j: (b, 0, 0))
    return pl.pallas_call(
        functools.partial(_attn_kernel, seq, top),
        out_shape=jax.ShapeDtypeStruct((BATCH, seq, D_ATTN), BF16),
        grid=(BATCH, seq // qb),
        in_specs=[per_q(D_ATTN), per_b(4 * LANES), per_b(KV_DIM), per_q(512), per_b(2 * LANES), per_q(LANES)],
        out_specs=per_q(D_ATTN),
        scratch_shapes=[
            pltpu.VMEM((N_KV_HEADS, HEAD_DIM + PACK16, seq), BF16),
            pltpu.VMEM((seq, qb), F32),
            pltpu.VMEM((seq, qb), BF16),
            pltpu.VMEM((seq, qb), jnp.int16),
            pltpu.VMEM((seq, qb), jnp.int16),
            pltpu.VMEM((seq, qb), BF16),
            pltpu.VMEM((seq, 4 * qb), F32),
            pltpu.VMEM((seq, 4 * qb), BF16),
        ],
        compiler_params=pltpu.CompilerParams(dimension_semantics=("parallel", "arbitrary"),
                                             vmem_limit_bytes=VMEM_LIMIT),
        name="dsa_attention",
    )(q, kz, v, qi, kiz, wi)


def _mix_ffn_kernel(ys_ref, ya_ref, h_ref, p_ref, wo_ref, g1_ref, b1_ref,
                    wup_ref, wd_ref, wpg_ref, wpp_ref, g2_ref, b2_ref, o_ref):
    mix = _dot(ys_ref[...], wo_ref[0:2 * D_S5, :]) + _dot(ya_ref[...], wo_ref[2 * D_S5:, :])
    h = _ln(ALPHA * h_ref[...] + mix, g1_ref[...], b1_ref[...])
    hb = h.astype(BF16)
    gate = _dot(hb, wup_ref[:, 0:D_FF])
    up = _dot(hb, wup_ref[:, D_FF:])
    act = (gate * _sigmoid(gate) * up).astype(BF16)
    ffn = _dot(act, wd_ref[...])
    ple = _sigmoid(_dot(hb, wpg_ref[...])) * _dot(p_ref[...].astype(BF16), wpp_ref[...])
    o_ref[...] = _ln(ALPHA * h + ffn + ple, g2_ref[...], b2_ref[...])


def _mix_ffn(layer, y_scan, y_at, h, p, w_out_p, g1, b1, w_up, w_down, w_pg, w_pp, g2, b2):
    t = h.shape[0]
    tm = min(t, 512)
    row = lambda c: pl.BlockSpec((tm, c), lambda i: (i, 0))
    const = lambda shape: pl.BlockSpec(shape, lambda i: (0, 0), pipeline_mode=pl.Buffered(1))
    of_layer = lambda r, c: pl.BlockSpec((pl.Squeezed(), r, c), lambda i: (layer, 0, 0),
                                         pipeline_mode=pl.Buffered(1))
    vec = _full_spec((1, D_MODEL))
    return pl.pallas_call(
        _mix_ffn_kernel,
        out_shape=jax.ShapeDtypeStruct((t, D_MODEL), F32),
        grid=(t // tm,),
        in_specs=[row(2 * D_S5), row(D_ATTN), row(D_MODEL), pl.BlockSpec((pl.Squeezed(), tm, D_PLE), lambda i: (layer, i, 0)),
                  const((D_MIX, D_MODEL)), vec, vec,
                  of_layer(D_MODEL, 2 * D_FF), of_layer(D_FF, D_MODEL), of_layer(D_MODEL, D_MODEL),
                  of_layer(D_PLE, D_MODEL), vec, vec],
        out_specs=row(D_MODEL),
        compiler_params=pltpu.CompilerParams(dimension_semantics=("parallel",),
                                             vmem_limit_bytes=VMEM_LIMIT),
        name="mix_ffn_ln",
    )(y_scan, y_at, h, p, w_out_p, g1, b1, w_up, w_down, w_pg, w_pp, g2, b2)


def _block_diag(w):
    hh, n, m = w.shape
    eye = jnp.eye(hh, dtype=w.dtype)
    return jnp.einsum('hij,hk->hikj', w, eye).reshape(hh * n, hh * m)


def _s5_out_matrix(c_re, c_im):
    cre = _block_diag(c_re.transpose(0, 2, 1))
    cim = _block_diag(c_im.transpose(0, 2, 1))
    return jnp.concatenate([cre, -cim], axis=0).astype(BF16)


def _attn_out_rows(w_at):
    w = w_at.reshape(N_KV_HEADS, 4, HEAD_DIM, D_MODEL)
    w = w[:, jnp.array([0, 2, 1, 3])]
    return w.transpose(1, 0, 2, 3).reshape(D_ATTN, D_MODEL)


def _forward(seq, top, x, p, positions, ln_emb_g, ln_emb_b, w_in,
             s5_lam_re, s5_lam_im, s5_log_step, s5_b_re, s5_b_im, s5_c_re, s5_c_im,
             s5_d, s5_w_glu, s5_b_glu,
             rg_conv_w, rg_conv_b, rg_wa, rg_ba, rg_wx, rg_bx, rg_lam,
             w_out, ln1_g, ln1_b, ffn_w_up, ffn_w_down, ple_w_gate, ple_w_proj,
             ln2_g, ln2_b):
    t = BATCH * seq
    cos, sin = _rope_tables(positions, seq)
    row = lambda a: a.reshape(1, -1)
    p_rows = p.reshape(DEPTH, t, D_PLE)
    w_up_b, w_down_b = ffn_w_up.astype(BF16), ffn_w_down.astype(BF16)
    w_pg_b, w_pp_b = ple_w_gate.astype(BF16), ple_w_proj.astype(BF16)
    h3 = x
    for i in range(DEPTH):
        w_in_p = jnp.pad(w_in[i], ((0, 0), (0, N_IN_PAD - N_IN))).astype(BF16)
        outs = _inproj(h3, w_in_p, cos, sin, row(ln_emb_g), row(ln_emb_b), i == 0, seq)
        if i == 0:
            h3, outs = outs[0], outs[1:]
        scan_in, q, kz, v, qi, kiz, wi = outs
        a8, bd = _s5_params(s5_lam_re[i], s5_lam_im[i], s5_log_step[i], s5_b_re[i], s5_b_im[i])
        y_scan = _scans(scan_in, a8, bd, _s5_out_matrix(s5_c_re[i], s5_c_im[i]), row(s5_d[i]),
                        s5_w_glu[i].astype(BF16), row(s5_b_glu[i]),
                        rg_conv_w[i], row(rg_conv_b[i]),
                        _block_diag(rg_wa[i]).astype(BF16), row(rg_ba[i]),
                        _block_diag(rg_wx[i]).astype(BF16), row(rg_bx[i]), row(rg_lam[i]), seq)
        y_at = _attention(q, kz, v, qi, kiz, wi, seq, top)
        w_out_p = jnp.concatenate([w_out[i][:2 * D_S5], _attn_out_rows(w_out[i][2 * D_S5:])], axis=0).astype(BF16)
        h2 = _mix_ffn(i, y_scan.reshape(t, 2 * D_S5), y_at.reshape(t, D_ATTN), h3.reshape(t, D_MODEL),
                      p_rows, w_out_p, row(ln1_g[i]), row(ln1_b[i]),
                      w_up_b, w_down_b, w_pg_b, w_pp_b, row(ln2_g[i]), row(ln2_b[i]))
        h3 = h2.reshape(BATCH, seq, D_MODEL)
    return h3


def kernel(x, p, positions, ln_emb_g, ln_emb_b, w_in, s5_lam_re, s5_lam_im, s5_log_step, s5_b_re, s5_b_im, s5_c_re, s5_c_im, s5_d, s5_w_glu, s5_b_glu, rg_conv_w, rg_conv_b, rg_wa, rg_ba, rg_wx, rg_bx, rg_lam, w_out, ln1_g, ln1_b, ffn_w_up, ffn_w_down, ple_w_gate, ple_w_proj, ln2_g, ln2_b):
    seq = x.shape[1]
    return _forward(seq, min(TOPK_MAX, seq // 4), x, p, positions, ln_emb_g, ln_emb_b, w_in,
                    s5_lam_re, s5_lam_im, s5_log_step, s5_b_re, s5_b_im, s5_c_re, s5_c_im,
                    s5_d, s5_w_glu, s5_b_glu,
                    rg_conv_w, rg_conv_b, rg_wa, rg_ba, rg_wx, rg_bx, rg_lam,
                    w_out, ln1_g, ln1_b, ffn_w_up, ffn_w_down, ple_w_gate, ple_w_proj,
                    ln2_g, ln2_b)
```

```python
import functools
import math

import jax
import jax.numpy as jnp
from jax import lax
from jax.experimental import pallas as pl
from jax.experimental.pallas import tpu as pltpu

F32 = jnp.float32
BF16 = jnp.bfloat16
I32 = jnp.int32

D_MODEL = 1024
BATCH = 8
SEQ = 2048
DEPTH = 2
D_S5 = 256
S5_GROUP = 16
S5_GROUPS = 16
S5_STATE = 64
N_S5_STATE = S5_GROUPS * S5_STATE
D_RG = 256
RG_BLOCKS = 8
RG_BLOCK = 32
RG_CONV = 4
RG_C = 8.0
N_HEADS = 8
N_KV_HEADS = 2
HEAD_DIM = 64
D_ATTN = 512
KV_DIM = 128
IDX_HEADS = 8
IDX_DIM = 64
TOPK_MAX = 256
D_MIX = 1024
ROPE_THETA = 10000.0
D_FF = 2816
D_PLE = 256
ALPHA = (2.0 * DEPTH) ** 0.25
LN_EPS = 1e-5
N_IN = 2120
N_IN_PAD = 2176
ATT_SCALE = HEAD_DIM ** -0.5
IDX_SCALE = (IDX_HEADS * IDX_DIM) ** -0.5
LOG2E = math.log2(math.e)

LANES = 128
SUBLANES = 8
PACK16 = 16
HALF16 = 32768
KV_CLASS = 256
ATT_CHUNK = 256
ATT_MM_CHUNK = 512
ATT_SUB = 64
NEG_INF_CODE = 127

VMEM_LIMIT = 56 * 1024 * 1024


def _dot(a, b):
    return jnp.dot(a, b, preferred_element_type=F32)


def _dot_nt(a, b):
    return lax.dot_general(a, b, (((1,), (1,)), ((), ())), preferred_element_type=F32)


def _ln(x, g, b):
    mu = jnp.mean(x, axis=-1, keepdims=True)
    xc = x - mu
    var = jnp.mean(xc * xc, axis=-1, keepdims=True)
    return xc * lax.rsqrt(var + LN_EPS) * g + b


def _gelu(x):
    c = math.sqrt(2.0 / math.pi)
    return 0.5 * x * (1.0 + jnp.tanh(c * (x + 0.044715 * (x * x * x))))


def _sigmoid(x):
    return 1.0 / (1.0 + jnp.exp(-x))


def _full_spec(shape):
    n = len(shape)
    return pl.BlockSpec(shape, lambda *_: (0,) * n)


def _rope_table_kernel(pos_ref, inv_ref, cos_ref, sin_ref):
    ang = pos_ref[...].astype(F32) * inv_ref[...]
    cos_ref[...] = jnp.cos(ang)
    sin_ref[...] = jnp.sin(ang)


def _rope_tables(positions, seq):
    t = BATCH * seq
    half = HEAD_DIM // 2
    per_row = LANES // half
    rows = t // per_row
    tm = min(rows, 2048)
    inv = ROPE_THETA ** (-jnp.arange(0, HEAD_DIM, 2, dtype=F32) / HEAD_DIM)
    inv128 = jnp.tile(inv, per_row)[None, :]
    pos = jnp.repeat(positions.reshape(rows, per_row), half, axis=1)
    cos, sin = pl.pallas_call(
        _rope_table_kernel,
        out_shape=(jax.ShapeDtypeStruct((rows, LANES), F32),) * 2,
        grid=(rows // tm,),
        in_specs=[pl.BlockSpec((tm, LANES), lambda i: (i, 0)), _full_spec((1, LANES))],
        out_specs=(pl.BlockSpec((tm, LANES), lambda i: (i, 0)),) * 2,
        compiler_params=pltpu.CompilerParams(dimension_semantics=("parallel",)),
        name="rope_tables",
    )(pos, inv128)
    cos = jnp.tile(cos.reshape(t, half), (1, per_row))
    sin = jnp.tile(sin.reshape(t, half), (1, per_row))
    return cos.reshape(BATCH, seq, LANES), sin.reshape(BATCH, seq, LANES)


def _inproj_kernel(apply_ln, tt, h_ref, w_ref, cos_ref, sin_ref, g_ref, b_ref, *outs):
    if apply_ln:
        hn_ref, so_ref, q_ref, kz_ref, v_ref, qi_ref, kiz_ref, wi_ref = outs
    else:
        so_ref, q_ref, kz_ref, v_ref, qi_ref, kiz_ref, wi_ref = outs
    rows = BATCH * tt
    h = h_ref[...].reshape(rows, D_MODEL)
    if apply_ln:
        h = _ln(h, g_ref[...], b_ref[...])
        hn_ref[...] = h.reshape(BATCH, tt, D_MODEL)
    hb = h.astype(BF16)
    cos = cos_ref[...].reshape(rows, LANES)
    sin = sin_ref[...].reshape(rows, LANES)
    lane = lax.broadcasted_iota(I32, (rows, LANES), 1)
    first_half = (lane & (HEAD_DIM // 2)) == 0
    lo64 = lane < HEAD_DIM

    def rope(x, cs, sn):
        partner = jnp.where(first_half, -pltpu.roll(x, LANES - 32, 1), pltpu.roll(x, 32, 1))
        return x * cs + partner * sn

    def rope_wide(x, n_chunks):
        return jnp.concatenate(
            [rope(x[:, c * LANES:(c + 1) * LANES], cos, sin) for c in range(n_chunks)], axis=1)

    ps = _dot(hb, w_ref[:, 0:768])
    for j in range(6):
        for b in range(BATCH):
            so_ref[j, pl.ds(b, tt, stride=BATCH), :] = ps[b * tt:(b + 1) * tt, j * LANES:(j + 1) * LANES]

    pq = _dot(hb, w_ref[:, 768:1280])
    q_ref[...] = (rope_wide(pq, 4) * (ATT_SCALE * LOG2E)).astype(BF16).reshape(BATCH, tt, D_ATTN)

    pkv = _dot(hb, w_ref[:, 1280:1536])
    kr = rope(pkv[:, 0:LANES], cos, sin)
    ksw = pltpu.roll(kr, HEAD_DIM, 1)
    zero = jnp.zeros_like(kr)
    kz = jnp.concatenate([jnp.where(lo64, kr, zero), jnp.where(lo64, zero, ksw),
                          jnp.where(lo64, ksw, zero), jnp.where(lo64, zero, kr)], axis=1)
    kz_ref[...] = kz.astype(BF16).reshape(BATCH, tt, 4 * LANES)
    v_ref[...] = pkv[:, LANES:2 * LANES].astype(BF16).reshape(BATCH, tt, KV_DIM)

    pqi = _dot(hb, w_ref[:, 1536:2048])
    qi_ref[...] = rope_wide(pqi, 4).astype(BF16).reshape(BATCH, tt, IDX_HEADS * IDX_DIM)

    pk = _dot(hb, w_ref[:, 2048:N_IN_PAD])
    kir = rope(pk, jnp.where(lo64, cos, 1.0), jnp.where(lo64, sin, 0.0))
    kie = jnp.where(lo64, kir, zero)
    kiz = jnp.concatenate([kie, pltpu.roll(kie, HEAD_DIM, 1)], axis=1)
    kiz_ref[...] = kiz.astype(BF16).reshape(BATCH, tt, 2 * LANES)
    wi_ref[...] = pk.reshape(BATCH, tt, LANES)


def _inproj(layer, h3, w_in_p, cos, sin, ln_g, ln_b, apply_ln, seq):
    tt = min(seq, 128)
    nt = seq // tt
    blk = lambda c: pl.BlockSpec((BATCH, tt, c), lambda i: (0, i, 0))
    out_shape = [
        jax.ShapeDtypeStruct((6, seq * BATCH, LANES), F32),
        jax.ShapeDtypeStruct((BATCH, seq, D_ATTN), BF16),
        jax.ShapeDtypeStruct((BATCH, seq, 4 * LANES), BF16),
        jax.ShapeDtypeStruct((BATCH, seq, KV_DIM), BF16),
        jax.ShapeDtypeStruct((BATCH, seq, IDX_HEADS * IDX_DIM), BF16),
        jax.ShapeDtypeStruct((BATCH, seq, 2 * LANES), BF16),
        jax.ShapeDtypeStruct((BATCH, seq, LANES), F32),
    ]
    out_specs = [pl.BlockSpec((6, tt * BATCH, LANES), lambda i: (0, i, 0)),
                 blk(D_ATTN), blk(4 * LANES), blk(KV_DIM), blk(512), blk(2 * LANES), blk(LANES)]
    if apply_ln:
        out_shape = [jax.ShapeDtypeStruct((BATCH, seq, D_MODEL), F32)] + out_shape
        out_specs = [blk(D_MODEL)] + out_specs
    return pl.pallas_call(
        functools.partial(_inproj_kernel, apply_ln, tt),
        out_shape=tuple(out_shape),
        grid=(nt,),
        in_specs=[blk(D_MODEL),
                  pl.BlockSpec((pl.Squeezed(), D_MODEL, N_IN_PAD), lambda i: (layer, 0, 0)),
                  blk(LANES), blk(LANES),
                  _full_spec((1, D_MODEL)), _full_spec((1, D_MODEL))],
        out_specs=tuple(out_specs),
        compiler_params=pltpu.CompilerParams(dimension_semantics=("parallel",),
                                             vmem_limit_bytes=VMEM_LIMIT),
        name="inproj_ln" if apply_ln else "inproj",
    )(h3, w_in_p, cos, sin, ln_g, ln_b)


def _s5_param_kernel(lr_ref, li_ref, ls_ref, br_ref, bi_ref, a_ref, bd_ref):
    lr = lr_ref[...]
    li = li_ref[...]
    step = jnp.exp(ls_ref[...])
    mag = jnp.exp(lr * step)
    ar = mag * jnp.cos(li * step)
    ai = mag * jnp.sin(li * step)
    den = lr * lr + li * li
    nr, ni = ar - 1.0, ai
    cr = (nr * lr + ni * li) / den
    ci = (ni * lr - nr * li) / den
    br = br_ref[...]
    bi = bi_ref[...]
    bbr = cr * br - ci * bi
    bbi = cr * bi + ci * br
    row = lax.broadcasted_iota(I32, (D_S5, N_S5_STATE), 0)
    col = lax.broadcasted_iota(I32, (D_S5, N_S5_STATE), 1)
    blk = (row // S5_GROUP) == (col // S5_STATE)
    zero = jnp.zeros((D_S5, N_S5_STATE), F32)
    bd_ref[:, 0:N_S5_STATE] = jnp.where(blk, jnp.concatenate([bbr] * S5_GROUPS, axis=0), zero).astype(BF16)
    bd_ref[:, N_S5_STATE:] = jnp.where(blk, jnp.concatenate([bbi] * S5_GROUPS, axis=0), zero).astype(BF16)
    a_ref[:, 0:N_S5_STATE] = jnp.broadcast_to(ar, (SUBLANES, N_S5_STATE))
    a_ref[:, N_S5_STATE:] = jnp.broadcast_to(ai, (SUBLANES, N_S5_STATE))


def _s5_params(lam_re, lam_im, log_step, b_re, b_im):
    lr = lam_re.reshape(1, N_S5_STATE)
    li = lam_im.reshape(1, N_S5_STATE)
    ls = jnp.repeat(log_step, S5_STATE).reshape(1, N_S5_STATE)
    br = b_re.transpose(2, 0, 1).reshape(S5_GROUP, N_S5_STATE)
    bi = b_im.transpose(2, 0, 1).reshape(S5_GROUP, N_S5_STATE)
    return pl.pallas_call(
        _s5_param_kernel,
        out_shape=(jax.ShapeDtypeStruct((SUBLANES, 2 * N_S5_STATE), F32),
                   jax.ShapeDtypeStruct((D_S5, 2 * N_S5_STATE), BF16)),
        name="s5_params",
    )(lr, li, ls, br, bi)


def _scan_kernel(tt, si_ref, a_ref, bd_ref, cd_ref, dsk_ref, wglu_ref, bglu_ref,
                 cw_ref, cb_ref, wa_ref, ba_ref, wx_ref, bx_ref, lam_ref,
                 o_ref, x_ref, hs_ref, halo_ref, ab_ref, hrg_ref, y_ref):
    rows = BATCH * tt
    i = pl.program_id(0)

    @pl.when(i == 0)
    def _init():
        hs_ref[...] = jnp.zeros_like(hs_ref)
        halo_ref[...] = jnp.zeros_like(halo_ref)
        hrg_ref[...] = jnp.zeros_like(hrg_ref)

    u = jnp.concatenate([si_ref[0], si_ref[1]], axis=1)
    x_ref[...] = _dot(u.astype(BF16), bd_ref[...])
    half = N_S5_STATE // 2
    for hh in range(2):
        lo = hh * half
        ar = a_ref[:, lo:lo + half]
        ai = a_ref[:, N_S5_STATE + lo:N_S5_STATE + lo + half]

        def step(t, carry, lo=lo, ar=ar, ai=ai):
            hr, hi = carry
            r0 = pl.multiple_of(t * BATCH, BATCH)
            br = x_ref[pl.ds(r0, BATCH), lo:lo + half]
            bi = x_ref[pl.ds(r0, BATCH), N_S5_STATE + lo:N_S5_STATE + lo + half]
            nr = ar * hr - ai * hi + br
            ni = ar * hi + ai * hr + bi
            x_ref[pl.ds(r0, BATCH), lo:lo + half] = nr
            x_ref[pl.ds(r0, BATCH), N_S5_STATE + lo:N_S5_STATE + lo + half] = ni
            return nr, ni

        hr, hi = lax.fori_loop(
            0, tt, step,
            (hs_ref[:, lo:lo + half], hs_ref[:, N_S5_STATE + lo:N_S5_STATE + lo + half]),
            unroll=True)
        hs_ref[:, lo:lo + half] = hr
        hs_ref[:, N_S5_STATE + lo:N_S5_STATE + lo + half] = hi

    y = _dot(x_ref[...].astype(BF16), cd_ref[...]) + dsk_ref[...] * u
    y = _gelu(y)
    y = y * _sigmoid(_dot(y.astype(BF16), wglu_ref[...]) + bglu_ref[...])
    y_ref[0] = y[:, 0:LANES]
    y_ref[1] = y[:, LANES:2 * LANES]

    xr = jnp.concatenate([si_ref[2], si_ref[3]], axis=1)
    gate = jnp.concatenate([si_ref[4], si_ref[5]], axis=1)
    hal = (RG_CONV - 1) * BATCH
    xext = jnp.concatenate([halo_ref[...], xr], axis=0)
    halo_ref[...] = xr[rows - hal:rows, :]
    xc = cb_ref[...]
    for k in range(RG_CONV):
        xc = xc + cw_ref[k:k + 1, :] * xext[k * BATCH:k * BATCH + rows, :]
    xcb = xc.astype(BF16)
    r = _sigmoid(_dot(xcb, wa_ref[...]) + ba_ref[...])
    ig = _sigmoid(_dot(xcb, wx_ref[...]) + bx_ref[...])
    nl = -lam_ref[...]
    softplus = jnp.maximum(nl, 0.0) + jnp.log(1.0 + jnp.exp(-jnp.abs(nl)))
    log_a = -RG_C * r * softplus
    a = jnp.exp(log_a)
    mult = jnp.sqrt(1.0 - a * a)
    ab_ref[0] = a
    ab_ref[1] = mult * (ig * xc)

    def rg_step(t, h):
        r0 = pl.multiple_of(t * BATCH, BATCH)
        hn = ab_ref[0, pl.ds(r0, BATCH), :] * h + ab_ref[1, pl.ds(r0, BATCH), :]
        ab_ref[1, pl.ds(r0, BATCH), :] = hn
        return hn

    hrg_ref[...] = lax.fori_loop(0, tt, rg_step, hrg_ref[...], unroll=True)
    yr = ab_ref[1] * _gelu(gate)
    y_ref[2] = yr[:, 0:LANES]
    y_ref[3] = yr[:, LANES:2 * LANES]

    for b in range(BATCH):
        o_ref[b] = jnp.concatenate(
            [y_ref[j, pl.ds(b, tt, stride=BATCH), :] for j in range(4)], axis=1).astype(BF16)


def _scans(scan_in, a8, bd, cd, dsk, wglu, bglu, cw, cb, wa, ba, wx, bx, lam, seq):
    tt = min(seq, 64)
    nt = seq // tt
    rows = tt * BATCH
    params = [a8, bd, cd, dsk, wglu, bglu, cw, cb, wa, ba, wx, bx, lam]
    return pl.pallas_call(
        functools.partial(_scan_kernel, tt),
        out_shape=jax.ShapeDtypeStruct((BATCH, seq, 2 * D_S5), BF16),
        grid=(nt,),
        in_specs=[pl.BlockSpec((6, rows, LANES), lambda i: (0, i, 0))] + [_full_spec(p.shape) for p in params],
        out_specs=pl.BlockSpec((BATCH, tt, 2 * D_S5), lambda i: (0, i, 0)),
        scratch_shapes=[
            pltpu.VMEM((rows, 2 * N_S5_STATE), F32),
            pltpu.VMEM((BATCH, 2 * N_S5_STATE), F32),
            pltpu.VMEM(((RG_CONV - 1) * BATCH, D_RG), F32),
            pltpu.VMEM((2, rows, D_RG), F32),
            pltpu.VMEM((BATCH, D_RG), F32),
            pltpu.VMEM((4, rows, LANES), F32),
        ],
        compiler_params=pltpu.CompilerParams(dimension_semantics=("arbitrary",),
                                             vmem_limit_bytes=VMEM_LIMIT),
        name="scans",
    )(scan_in, *params)


def _count16(ref, skv, cand, op):
    cb = jnp.broadcast_to(cand, (PACK16, LANES))
    accs = []
    for i in range(skv // PACK16):
        one = jnp.where(op(ref[i * PACK16:(i + 1) * PACK16, :], cb), jnp.int16(1), jnp.int16(0))
        if i < 4:
            accs.append(one)
        else:
            accs[i % 4] = accs[i % 4] + one
    tot = (accs[0] + accs[1]) + (accs[2] + accs[3])
    return jnp.sum(tot.astype(I32), axis=0, keepdims=True)


def _attn_kernel(seq, top, q_ref, kz_ref, v_ref, qi_ref, kiz_ref, wi_ref, o_ref,
                 vt_ref, sc_ref, sb_ref, lom_ref, tie_ref, bias_ref, l_ref, p_ref):
    j = pl.program_id(1)

    @pl.when(j == 0)
    def _():
        vt = v_ref[...].astype(F32).T.astype(BF16)
        ones = jnp.where(lax.broadcasted_iota(I32, (PACK16, seq), 0) == 0, 1.0, 0.0).astype(BF16)
        for g in range(N_KV_HEADS):
            vt_ref[g, 0:HEAD_DIM, :] = vt[g * HEAD_DIM:(g + 1) * HEAD_DIM, :]
            vt_ref[g, HEAD_DIM:HEAD_DIM + PACK16, :] = ones

    cls_rows = min(seq, KV_CLASS)
    for c in range(seq // cls_rows):
        @pl.when(j // (cls_rows // LANES) == c)
        def _(c=c):
            _attn_class(cls_rows * (c + 1), top, j, q_ref, kz_ref, qi_ref, kiz_ref, wi_ref, o_ref,
                        vt_ref, sc_ref, sb_ref, lom_ref, tie_ref, bias_ref, l_ref, p_ref)


def _attn_class(skv, top, j, q_ref, kz_ref, qi_ref, kiz_ref, wi_ref, o_ref,
                vt_ref, sc_ref, sb_ref, lom_ref, tie_ref, bias_ref, l_ref, p_ref):
    ck = ATT_CHUNK
    nc = skv // ck
    mm_chunks = [(r0, min(ATT_MM_CHUNK, skv - r0)) for r0 in range(0, skv, ATT_MM_CHUNK)]
    sub = ATT_SUB
    qb = LANES
    t_idx = j * qb + lax.broadcasted_iota(I32, (ck, qb), 1)
    s_iota = lax.broadcasted_iota(I32, (ck, qb), 0)
    t_sub = j * qb + lax.broadcasted_iota(I32, (sub, qb), 1)
    s_sub = lax.broadcasted_iota(I32, (sub, qb), 0)

    def transposed_chunks(x):
        xf = x.astype(F32)
        return jnp.concatenate([xf[:, c * LANES:(c + 1) * LANES].T for c in range(4)], axis=1).astype(BF16)

    wts = wi_ref[...].T[HEAD_DIM:HEAD_DIM + IDX_HEADS, :] * IDX_SCALE
    qit = transposed_chunks(qi_ref[...])
    for r0, mk in mm_chunks:
        se = _dot(kiz_ref[r0:r0 + mk, 0:LANES], qit)
        so = _dot(kiz_ref[r0:r0 + mk, LANES:2 * LANES], qit)
        for s0 in range(0, mk, sub):
            acc = None
            for c4 in range(4):
                te = wts[2 * c4:2 * c4 + 1, :] * jnp.maximum(se[s0:s0 + sub, c4 * qb:(c4 + 1) * qb], 0.0)
                to = wts[2 * c4 + 1:2 * c4 + 2, :] * jnp.maximum(so[s0:s0 + sub, c4 * qb:(c4 + 1) * qb], 0.0)
                acc = te + to if acc is None else acc + (te + to)
            rows = slice(r0 + s0, r0 + s0 + sub)
            causal = (r0 + s0) + s_sub <= t_sub
            sc = jnp.where(causal, acc, -jnp.inf)
            sc_ref[rows, :] = sc
            near = sc.astype(BF16)
            rounded_up = (sc - near.astype(F32)).astype(BF16) < 0
            nb = pltpu.bitcast(near, jnp.int16)
            below = pltpu.bitcast(nb + jnp.where(nb < 0, jnp.int16(1), jnp.int16(-1)), BF16)
            sb_ref[rows, :] = jnp.where(rounded_up, below, near)

    ge = lambda a, b: a >= b
    lt = lambda a, b: a < b

    def f32_of_code(u):
        k = u - HALF16
        return pltpu.bitcast(jnp.left_shift(jnp.where(k < 0, k ^ 0x7FFF, k), 16), F32)

    def coarse_body(i, tu):
        cand = jnp.minimum(tu | jnp.left_shift(jnp.int32(1), 15 - i), 65535 - NEG_INF_CODE)
        cnt = _count16(sb_ref, skv, f32_of_code(cand + NEG_INF_CODE).astype(BF16), ge)
        return jnp.where(cnt >= top, cand, tu)

    lo_code = lax.fori_loop(0, 16, coarse_body, jnp.zeros((1, qb), I32)) + NEG_INF_CODE
    lo_f = f32_of_code(lo_code)
    hi_f = f32_of_code(lo_code + 1)
    lo_b = lo_f.astype(BF16)
    hi_b = hi_f.astype(BF16)
    keep, drop = jnp.zeros((), BF16), jnp.full((), -jnp.inf, BF16)
    n_above = _count16(sb_ref, skv, hi_b, ge)
    width = hi_f - lo_f
    usable = jnp.logical_and(width > 1e-30, width < 1e30)
    scale = jnp.where(usable, 65536.0 / jnp.where(usable, width, 1.0), 0.0)
    outside = jnp.int16(-HALF16)

    def residuals(c, carry):
        r0 = pl.multiple_of(c * ck, ck)
        s = sc_ref[pl.ds(r0, ck), :]
        x = jnp.clip(jnp.where(s > lo_f, s - lo_f, 0.0) * scale, 0.0, 65534.0)
        rq = (x.astype(I32) - (HALF16 - 1)).astype(jnp.int16)
        sb = sb_ref[pl.ds(r0, ck), :]
        inside = jnp.logical_and(sb >= lo_b, sb < hi_b)
        lom_ref[pl.ds(r0, ck), :] = jnp.where(inside, rq, outside)
        return carry

    lax.fori_loop(0, nc, residuals, 0)

    def fine_body(i, tu):
        cand = tu | jnp.left_shift(jnp.int32(1), 15 - i)
        cnt = _count16(lom_ref, skv, (cand - HALF16).astype(jnp.int16), ge)
        return jnp.where(cnt >= top - n_above, cand, tu)

    cut = (lax.fori_loop(0, 16, fine_body, jnp.zeros((1, qb), I32)) - HALF16).astype(jnp.int16)

    def bias_default(c, acc):
        r0 = pl.multiple_of(c * ck, ck)
        at_cut = lom_ref[pl.ds(r0, ck), :] >= cut
        sel = jnp.logical_or(sb_ref[pl.ds(r0, ck), :] >= hi_b, at_cut)
        bias_ref[pl.ds(r0, ck), :] = jnp.where(sel, keep, drop)
        ones = jnp.where(at_cut, jnp.int16(1), jnp.int16(0))
        parts = [ones[i * PACK16:(i + 1) * PACK16, :] for i in range(ck // PACK16)]
        while len(parts) > 1:
            parts = [a + b for a, b in zip(parts[0::2], parts[1::2])]
        return acc + parts[0]

    n_cut = lax.fori_loop(0, nc, bias_default, jnp.zeros((PACK16, qb), jnp.int16))
    n_ge = n_above + jnp.sum(n_cut.astype(I32), axis=0, keepdims=True)
    tie = jnp.logical_and(n_ge > top, lo_code != NEG_INF_CODE)
    any_tie = jnp.max(jnp.where(tie, 1, 0).astype(I32)) > 0

    @pl.when(any_tie)
    def _tie():
        nbits = (skv - 1).bit_length()
        n_over = _count16(lom_ref, skv, cut, lambda a, b: a > b)
        need = top - n_above - n_over
        big = jnp.int16(2 ** nbits)

        def fill(c, carry):
            r0 = pl.multiple_of(c * ck, ck)
            idx = (r0 + s_iota).astype(jnp.int16)
            tie_ref[pl.ds(r0, ck), :] = jnp.where(lom_ref[pl.ds(r0, ck), :] == cut, idx, big)
            return carry

        lax.fori_loop(0, nc, fill, 0)

        def bit2(i, m):
            cand = m | jnp.left_shift(jnp.int32(1), nbits - 1 - i)
            cnt = _count16(tie_ref, skv, cand.astype(jnp.int16), lt)
            return jnp.where(cnt < need, cand, m)

        m = lax.fori_loop(0, nbits, bit2, jnp.zeros((1, qb), I32)).astype(jnp.int16)

        def bias_tie(c, carry):
            r0 = pl.multiple_of(c * ck, ck)
            sel = jnp.logical_or(sb_ref[pl.ds(r0, ck), :] >= hi_b,
                                 lom_ref[pl.ds(r0, ck), :] > cut)
            sel = jnp.logical_or(sel, tie_ref[pl.ds(r0, ck), :] <= m)
            bias_ref[pl.ds(r0, ck), :] = jnp.where(sel, keep, drop)
            return carry

        lax.fori_loop(0, nc, bias_tie, 0)

    qt = transposed_chunks(q_ref[...])
    o_parts = []
    for g in range(N_KV_HEADS):
        qet = qt[:, (2 * g) * qb:(2 * g + 2) * qb]
        m8 = jnp.full((SUBLANES, 4 * qb), -jnp.inf, F32)
        for r0, mk in mm_chunks:
            le = _dot(kz_ref[r0:r0 + mk, (2 * g) * LANES:(2 * g + 1) * LANES], qet)
            lo = _dot(kz_ref[r0:r0 + mk, (2 * g + 1) * LANES:(2 * g + 2) * LANES], qet)
            for s0 in range(0, mk, sub):
                rows = slice(r0 + s0, r0 + s0 + sub)
                b = bias_ref[rows, :].astype(F32)
                if r0 + s0 + sub > skv - KV_CLASS:
                    b = jnp.where((r0 + s0) + s_sub <= t_sub, b, -jnp.inf)
                l = jnp.concatenate([le[s0:s0 + sub] + jnp.concatenate([b, b], axis=1),
                                     lo[s0:s0 + sub] + jnp.concatenate([b, b], axis=1)], axis=1)
                l_ref[rows, :] = l
                m8 = jnp.maximum(m8, jnp.max(l.reshape(sub // SUBLANES, SUBLANES, 4 * qb), axis=0))
        mx = jnp.max(m8, axis=0, keepdims=True)

        def pbody(c, carry, mx=mx):
            r0 = pl.multiple_of(c * ck, ck)
            p_ref[pl.ds(r0, ck), :] = jnp.exp2((l_ref[pl.ds(r0, ck), :] - mx).astype(BF16))
            return carry

        lax.fori_loop(0, nc, pbody, 0)
        ot = _dot(vt_ref[g, :, 0:skv], p_ref[0:skv, :])
        o_parts.append(ot[0:HEAD_DIM, :] * (1.0 / ot[HEAD_DIM:HEAD_DIM + 1, :]))
    ot = jnp.concatenate(o_parts, axis=0)
    y = jnp.concatenate([ot[:, i * qb:(i + 1) * qb].T for i in range(4)], axis=1)
    o_ref[...] = y.astype(BF16)


def _attention(q, kz, v, qi, kiz, wi, seq, top):
    qb = LANES
    sq = pl.Squeezed()
    per_q = lambda c: pl.BlockSpec((sq, qb, c), lambda b, j: (b, j, 0))
    per_b = lambda c: pl.BlockSpec((sq, seq, c), lambda b, j: (b, 0, 0))
    return pl.pallas_call(
        functools.partial(_attn_kernel, seq, top),
        out_shape=jax.ShapeDtypeStruct((BATCH, seq, D_ATTN), BF16),
        grid=(BATCH, seq // qb),
        in_specs=[per_q(D_ATTN), per_b(4 * LANES), per_b(KV_DIM), per_q(512), per_b(2 * LANES), per_q(LANES)],
        out_specs=per_q(D_ATTN),
        scratch_shapes=[
            pltpu.VMEM((N_KV_HEADS, HEAD_DIM + PACK16, seq), BF16),
            pltpu.VMEM((seq, qb), F32),
            pltpu.VMEM((seq, qb), BF16),
            pltpu.VMEM((seq, qb), jnp.int16),
            pltpu.VMEM((seq, qb), jnp.int16),
            pltpu.VMEM((seq, qb), BF16),
            pltpu.VMEM((seq, 4 * qb), F32),
            pltpu.VMEM((seq, 4 * qb), BF16),
        ],
        compiler_params=pltpu.CompilerParams(dimension_semantics=("parallel", "arbitrary"),
                                             vmem_limit_bytes=VMEM_LIMIT),
        name="dsa_attention",
    )(q, kz, v, qi, kiz, wi)


def _mix_ffn_kernel(ys_ref, ya_ref, h_ref, p_ref, wo_ref, g1_ref, b1_ref,
                    wup_ref, wd_ref, wpg_ref, wpp_ref, g2_ref, b2_ref, o_ref):
    mix = _dot(ys_ref[...], wo_ref[0:2 * D_S5, :]) + _dot(ya_ref[...], wo_ref[2 * D_S5:, :])
    h = _ln(ALPHA * h_ref[...] + mix, g1_ref[...], b1_ref[...])
    hb = h.astype(BF16)
    gate = _dot(hb, wup_ref[:, 0:D_FF])
    up = _dot(hb, wup_ref[:, D_FF:])
    act = (gate * _sigmoid(gate) * up).astype(BF16)
    ffn = _dot(act, wd_ref[...])
    ple = _sigmoid(_dot(hb, wpg_ref[...])) * _dot(p_ref[...].astype(BF16), wpp_ref[...])
    o_ref[...] = _ln(ALPHA * h + ffn + ple, g2_ref[...], b2_ref[...])


def _mix_ffn(layer, y_scan, y_at, h, p, w_out_p, g1, b1, w_up, w_down, w_pg, w_pp, g2, b2):
    t = h.shape[0]
    tm = min(t, 512)
    row = lambda c: pl.BlockSpec((tm, c), lambda i: (i, 0))
    const = lambda shape: pl.BlockSpec(shape, lambda i: (0, 0), pipeline_mode=pl.Buffered(1))
    of_layer = lambda r, c: pl.BlockSpec((pl.Squeezed(), r, c), lambda i: (layer, 0, 0),
                                         pipeline_mode=pl.Buffered(1))
    vec = _full_spec((1, D_MODEL))
    return pl.pallas_call(
        _mix_ffn_kernel,
        out_shape=jax.ShapeDtypeStruct((t, D_MODEL), F32),
        grid=(t // tm,),
        in_specs=[row(2 * D_S5), row(D_ATTN), row(D_MODEL), pl.BlockSpec((pl.Squeezed(), tm, D_PLE), lambda i: (layer, i, 0)),
                  const((D_MIX, D_MODEL)), vec, vec,
                  of_layer(D_MODEL, 2 * D_FF), of_layer(D_FF, D_MODEL), of_layer(D_MODEL, D_MODEL),
                  of_layer(D_PLE, D_MODEL), vec, vec],
        out_specs=row(D_MODEL),
        compiler_params=pltpu.CompilerParams(dimension_semantics=("parallel",),
                                             vmem_limit_bytes=VMEM_LIMIT),
        name="mix_ffn_ln",
    )(y_scan, y_at, h, p, w_out_p, g1, b1, w_up, w_down, w_pg, w_pp, g2, b2)


def _block_diag(w):
    hh, n, m = w.shape
    eye = jnp.eye(hh, dtype=w.dtype)
    return jnp.einsum('hij,hk->hikj', w, eye).reshape(hh * n, hh * m)


def _s5_out_matrix(c_re, c_im):
    cre = _block_diag(c_re.transpose(0, 2, 1))
    cim = _block_diag(c_im.transpose(0, 2, 1))
    return jnp.concatenate([cre, -cim], axis=0).astype(BF16)


def _attn_out_rows(w_at):
    w = w_at.reshape(N_KV_HEADS, 4, HEAD_DIM, D_MODEL)
    w = w[:, jnp.array([0, 2, 1, 3])]
    return w.transpose(1, 0, 2, 3).reshape(D_ATTN, D_MODEL)


def _forward(seq, top, x, p, positions, ln_emb_g, ln_emb_b, w_in,
             s5_lam_re, s5_lam_im, s5_log_step, s5_b_re, s5_b_im, s5_c_re, s5_c_im,
             s5_d, s5_w_glu, s5_b_glu,
             rg_conv_w, rg_conv_b, rg_wa, rg_ba, rg_wx, rg_bx, rg_lam,
             w_out, ln1_g, ln1_b, ffn_w_up, ffn_w_down, ple_w_gate, ple_w_proj,
             ln2_g, ln2_b):
    t = BATCH * seq
    cos, sin = _rope_tables(positions, seq)
    row = lambda a: a.reshape(1, -1)
    p_rows = p.reshape(DEPTH, t, D_PLE)
    w_in_p = jnp.pad(w_in.astype(BF16), ((0, 0), (0, 0), (0, N_IN_PAD - N_IN)))
    w_up_b, w_down_b = ffn_w_up.astype(BF16), ffn_w_down.astype(BF16)
    w_pg_b, w_pp_b = ple_w_gate.astype(BF16), ple_w_proj.astype(BF16)
    h3 = x
    for i in range(DEPTH):
        outs = _inproj(i, h3, w_in_p, cos, sin, row(ln_emb_g), row(ln_emb_b), i == 0, seq)
        if i == 0:
            h3, outs = outs[0], outs[1:]
        scan_in, q, kz, v, qi, kiz, wi = outs
        a8, bd = _s5_params(s5_lam_re[i], s5_lam_im[i], s5_log_step[i], s5_b_re[i], s5_b_im[i])
        y_scan = _scans(scan_in, a8, bd, _s5_out_matrix(s5_c_re[i], s5_c_im[i]), row(s5_d[i]),
                        s5_w_glu[i].astype(BF16), row(s5_b_glu[i]),
                        rg_conv_w[i], row(rg_conv_b[i]),
                        _block_diag(rg_wa[i]).astype(BF16), row(rg_ba[i]),
                        _block_diag(rg_wx[i]).astype(BF16), row(rg_bx[i]), row(rg_lam[i]), seq)
        y_at = _attention(q, kz, v, qi, kiz, wi, seq, top)
        w_out_p = jnp.concatenate([w_out[i][:2 * D_S5], _attn_out_rows(w_out[i][2 * D_S5:])], axis=0).astype(BF16)
        h2 = _mix_ffn(i, y_scan.reshape(t, 2 * D_S5), y_at.reshape(t, D_ATTN), h3.reshape(t, D_MODEL),
                      p_rows, w_out_p, row(ln1_g[i]), row(ln1_b[i]),
                      w_up_b, w_down_b, w_pg_b, w_pp_b, row(ln2_g[i]), row(ln2_b[i]))
        h3 = h2.reshape(BATCH, seq, D_MODEL)
    return h3


def kernel(x, p, positions, ln_emb_g, ln_emb_b, w_in, s5_lam_re, s5_lam_im, s5_log_step, s5_b_re, s5_b_im, s5_c_re, s5_c_im, s5_d, s5_w_glu, s5_b_glu, rg_conv_w, rg_conv_b, rg_wa, rg_ba, rg_wx, rg_bx, rg_lam, w_out, ln1_g, ln1_b, ffn_w_up, ffn_w_down, ple_w_gate, ple_w_proj, ln2_g, ln2_b):
    seq = x.shape[1]
    return _forward(seq, min(TOPK_MAX, seq // 4), x, p, positions, ln_emb_g, ln_emb_b, w_in,
                    s5_lam_re, s5_lam_im, s5_log_step, s5_b_re, s5_b_im, s5_c_re, s5_c_im,
                    s5_d, s5_w_glu, s5_b_glu,
                    rg_conv_w, rg_conv_b, rg_wa, rg_ba, rg_wx, rg_bx, rg_lam,
                    w_out, ln1_g, ln1_b, ffn_w_up, ffn_w_down, ple_w_gate, ple_w_proj,
                    ln2_g, ln2_b)
```

```python
import functools
import math

import jax
import jax.numpy as jnp
from jax import lax
from jax.experimental import pallas as pl
from jax.experimental.pallas import tpu as pltpu

F32 = jnp.float32
BF16 = jnp.bfloat16
I32 = jnp.int32

D_MODEL = 1024
BATCH = 8
SEQ = 2048
DEPTH = 2
D_S5 = 256
S5_GROUP = 16
S5_GROUPS = 16
S5_STATE = 64
N_S5_STATE = S5_GROUPS * S5_STATE
D_RG = 256
RG_BLOCKS = 8
RG_BLOCK = 32
RG_CONV = 4
RG_C = 8.0
N_HEADS = 8
N_KV_HEADS = 2
HEAD_DIM = 64
D_ATTN = 512
KV_DIM = 128
IDX_HEADS = 8
IDX_DIM = 64
TOPK_MAX = 256
D_MIX = 1024
ROPE_THETA = 10000.0
D_FF = 2816
D_PLE = 256
ALPHA = (2.0 * DEPTH) ** 0.25
LN_EPS = 1e-5
N_IN = 2120
N_IN_PAD = 2176
ATT_SCALE = HEAD_DIM ** -0.5
IDX_SCALE = (IDX_HEADS * IDX_DIM) ** -0.5
LOG2E = math.log2(math.e)

LANES = 128
SUBLANES = 8
PACK16 = 16
HALF16 = 32768
KV_CLASS = 256
ATT_CHUNK = 256
ATT_MM_CHUNK = 512
ATT_SUB = 64
NEG_INF_CODE = 127

VMEM_LIMIT = 56 * 1024 * 1024


def _dot(a, b):
    return jnp.dot(a, b, preferred_element_type=F32)


def _dot_nt(a, b):
    return lax.dot_general(a, b, (((1,), (1,)), ((), ())), preferred_element_type=F32)


def _ln(x, g, b):
    mu = jnp.mean(x, axis=-1, keepdims=True)
    xc = x - mu
    var = jnp.mean(xc * xc, axis=-1, keepdims=True)
    return xc * lax.rsqrt(var + LN_EPS) * g + b


def _gelu(x):
    c = math.sqrt(2.0 / math.pi)
    return 0.5 * x * (1.0 + jnp.tanh(c * (x + 0.044715 * (x * x * x))))


def _sigmoid(x):
    return 1.0 / (1.0 + jnp.exp(-x))


def _full_spec(shape):
    n = len(shape)
    return pl.BlockSpec(shape, lambda *_: (0,) * n)


def _rope_table_kernel(pos_ref, inv_ref, sgn_ref, cos_ref, sin_ref):
    ang = pos_ref[...].astype(F32) * inv_ref[...]
    cos_ref[...] = jnp.cos(ang)
    sin_ref[...] = jnp.sin(ang) * sgn_ref[...]


def _rope_tables(positions, seq):
    t = BATCH * seq
    tm = min(t, 2048)
    inv = ROPE_THETA ** (-jnp.arange(0, HEAD_DIM, 2, dtype=F32) / HEAD_DIM)
    inv128 = jnp.tile(inv, 4)[None, :]
    sgn = jnp.where((jnp.arange(LANES) % HEAD_DIM) < HEAD_DIM // 2, -1.0, 1.0).astype(F32)[None, :]
    pos = positions.reshape(t, 1)
    cos, sin = pl.pallas_call(
        _rope_table_kernel,
        out_shape=(jax.ShapeDtypeStruct((t, LANES), F32),) * 2,
        grid=(t // tm,),
        in_specs=[pl.BlockSpec((tm, 1), lambda i: (i, 0)), _full_spec((1, LANES)), _full_spec((1, LANES))],
        out_specs=(pl.BlockSpec((tm, LANES), lambda i: (i, 0)),) * 2,
        compiler_params=pltpu.CompilerParams(dimension_semantics=("parallel",)),
        name="rope_tables",
    )(pos, inv128, sgn)
    return cos.reshape(BATCH, seq, LANES), sin.reshape(BATCH, seq, LANES)


def _inproj_kernel(apply_ln, tt, h_ref, w_ref, cos_ref, sin_ref, g_ref, b_ref, *outs):
    if apply_ln:
        hn_ref, so_ref, q_ref, kz_ref, v_ref, qi_ref, kiz_ref, wi_ref = outs
    else:
        so_ref, q_ref, kz_ref, v_ref, qi_ref, kiz_ref, wi_ref = outs
    rows = BATCH * tt
    h = h_ref[...].reshape(rows, D_MODEL)
    if apply_ln:
        h = _ln(h, g_ref[...], b_ref[...])
        hn_ref[...] = h.reshape(BATCH, tt, D_MODEL)
    hb = h.astype(BF16)
    cos = cos_ref[...].reshape(rows, LANES)
    sin = sin_ref[...].reshape(rows, LANES)
    lane = lax.broadcasted_iota(I32, (rows, LANES), 1)
    first_half = (lane & (HEAD_DIM // 2)) == 0
    lo64 = lane < HEAD_DIM

    def rope(x, cs, sn):
        partner = jnp.where(first_half, pltpu.roll(x, LANES - 32, 1), pltpu.roll(x, 32, 1))
        return x * cs + partner * sn

    def rope_wide(x, n_chunks):
        return jnp.concatenate(
            [rope(x[:, c * LANES:(c + 1) * LANES], cos, sin) for c in range(n_chunks)], axis=1)

    ps = _dot(hb, w_ref[:, 0:768])
    for j in range(6):
        for b in range(BATCH):
            so_ref[j, pl.ds(b, tt, stride=BATCH), :] = ps[b * tt:(b + 1) * tt, j * LANES:(j + 1) * LANES]

    pq = _dot(hb, w_ref[:, 768:1280])
    q_ref[...] = (rope_wide(pq, 4) * (ATT_SCALE * LOG2E)).astype(BF16).reshape(BATCH, tt, D_ATTN)

    pkv = _dot(hb, w_ref[:, 1280:1536])
    kr = rope(pkv[:, 0:LANES], cos, sin)
    ksw = pltpu.roll(kr, HEAD_DIM, 1)
    zero = jnp.zeros_like(kr)
    kz = jnp.concatenate([jnp.where(lo64, kr, zero), jnp.where(lo64, zero, ksw),
                          jnp.where(lo64, ksw, zero), jnp.where(lo64, zero, kr)], axis=1)
    kz_ref[...] = kz.astype(BF16).reshape(BATCH, tt, 4 * LANES)
    v_ref[...] = pkv[:, LANES:2 * LANES].astype(BF16).reshape(BATCH, tt, KV_DIM)

    pqi = _dot(hb, w_ref[:, 1536:2048])
    qi_ref[...] = rope_wide(pqi, 4).astype(BF16).reshape(BATCH, tt, IDX_HEADS * IDX_DIM)

    pk = _dot(hb, w_ref[:, 2048:N_IN_PAD])
    kir = rope(pk, jnp.where(lo64, cos, 1.0), jnp.where(lo64, sin, 0.0))
    kie = jnp.where(lo64, kir, zero)
    kiz = jnp.concatenate([kie, pltpu.roll(kie, HEAD_DIM, 1)], axis=1)
    kiz_ref[...] = kiz.astype(BF16).reshape(BATCH, tt, 2 * LANES)
    wi_ref[...] = pk.reshape(BATCH, tt, LANES)


def _inproj(h3, w_in_p, cos, sin, ln_g, ln_b, apply_ln, seq):
    tt = min(seq, 128)
    nt = seq // tt
    blk = lambda c: pl.BlockSpec((BATCH, tt, c), lambda i: (0, i, 0))
    out_shape = [
        jax.ShapeDtypeStruct((6, seq * BATCH, LANES), F32),
        jax.ShapeDtypeStruct((BATCH, seq, D_ATTN), BF16),
        jax.ShapeDtypeStruct((BATCH, seq, 4 * LANES), BF16),
        jax.ShapeDtypeStruct((BATCH, seq, KV_DIM), BF16),
        jax.ShapeDtypeStruct((BATCH, seq, IDX_HEADS * IDX_DIM), BF16),
        jax.ShapeDtypeStruct((BATCH, seq, 2 * LANES), BF16),
        jax.ShapeDtypeStruct((BATCH, seq, LANES), F32),
    ]
    out_specs = [pl.BlockSpec((6, tt * BATCH, LANES), lambda i: (0, i, 0)),
                 blk(D_ATTN), blk(4 * LANES), blk(KV_DIM), blk(512), blk(2 * LANES), blk(LANES)]
    if apply_ln:
        out_shape = [jax.ShapeDtypeStruct((BATCH, seq, D_MODEL), F32)] + out_shape
        out_specs = [blk(D_MODEL)] + out_specs
    return pl.pallas_call(
        functools.partial(_inproj_kernel, apply_ln, tt),
        out_shape=tuple(out_shape),
        grid=(nt,),
        in_specs=[blk(D_MODEL), _full_spec((D_MODEL, N_IN_PAD)), blk(LANES), blk(LANES),
                  _full_spec((1, D_MODEL)), _full_spec((1, D_MODEL))],
        out_specs=tuple(out_specs),
        compiler_params=pltpu.CompilerParams(dimension_semantics=("parallel",),
                                             vmem_limit_bytes=VMEM_LIMIT),
        name="inproj_ln" if apply_ln else "inproj",
    )(h3, w_in_p, cos, sin, ln_g, ln_b)


def _s5_param_kernel(lr_ref, li_ref, ls_ref, br_ref, bi_ref, a_ref, bd_ref):
    lr = lr_ref[...]
    li = li_ref[...]
    step = jnp.exp(ls_ref[...])
    mag = jnp.exp(lr * step)
    ar = mag * jnp.cos(li * step)
    ai = mag * jnp.sin(li * step)
    den = lr * lr + li * li
    nr, ni = ar - 1.0, ai
    cr = (nr * lr + ni * li) / den
    ci = (ni * lr - nr * li) / den
    br = br_ref[...]
    bi = bi_ref[...]
    bbr = cr * br - ci * bi
    bbi = cr * bi + ci * br
    row = lax.broadcasted_iota(I32, (D_S5, N_S5_STATE), 0)
    col = lax.broadcasted_iota(I32, (D_S5, N_S5_STATE), 1)
    blk = (row // S5_GROUP) == (col // S5_STATE)
    zero = jnp.zeros((D_S5, N_S5_STATE), F32)
    bd_ref[:, 0:N_S5_STATE] = jnp.where(blk, jnp.concatenate([bbr] * S5_GROUPS, axis=0), zero).astype(BF16)
    bd_ref[:, N_S5_STATE:] = jnp.where(blk, jnp.concatenate([bbi] * S5_GROUPS, axis=0), zero).astype(BF16)
    a_ref[:, 0:N_S5_STATE] = jnp.broadcast_to(ar, (SUBLANES, N_S5_STATE))
    a_ref[:, N_S5_STATE:] = jnp.broadcast_to(ai, (SUBLANES, N_S5_STATE))


def _s5_params(lam_re, lam_im, log_step, b_re, b_im):
    lr = lam_re.reshape(1, N_S5_STATE)
    li = lam_im.reshape(1, N_S5_STATE)
    ls = jnp.repeat(log_step, S5_STATE).reshape(1, N_S5_STATE)
    br = b_re.transpose(2, 0, 1).reshape(S5_GROUP, N_S5_STATE)
    bi = b_im.transpose(2, 0, 1).reshape(S5_GROUP, N_S5_STATE)
    return pl.pallas_call(
        _s5_param_kernel,
        out_shape=(jax.ShapeDtypeStruct((SUBLANES, 2 * N_S5_STATE), F32),
                   jax.ShapeDtypeStruct((D_S5, 2 * N_S5_STATE), BF16)),
        name="s5_params",
    )(lr, li, ls, br, bi)


def _scan_kernel(tt, si_ref, a_ref, bd_ref, cd_ref, dsk_ref, wglu_ref, bglu_ref,
                 cw_ref, cb_ref, wa_ref, ba_ref, wx_ref, bx_ref, lam_ref,
                 o_ref, x_ref, hs_ref, halo_ref, ab_ref, hrg_ref, y_ref):
    rows = BATCH * tt
    i = pl.program_id(0)

    @pl.when(i == 0)
    def _init():
        hs_ref[...] = jnp.zeros_like(hs_ref)
        halo_ref[...] = jnp.zeros_like(halo_ref)
        hrg_ref[...] = jnp.zeros_like(hrg_ref)

    u = jnp.concatenate([si_ref[0], si_ref[1]], axis=1)
    x_ref[...] = _dot(u.astype(BF16), bd_ref[...])
    half = N_S5_STATE // 2
    for hh in range(2):
        lo = hh * half
        ar = a_ref[:, lo:lo + half]
        ai = a_ref[:, N_S5_STATE + lo:N_S5_STATE + lo + half]

        def step(t, carry, lo=lo, ar=ar, ai=ai):
            hr, hi = carry
            r0 = pl.multiple_of(t * BATCH, BATCH)
            br = x_ref[pl.ds(r0, BATCH), lo:lo + half]
            bi = x_ref[pl.ds(r0, BATCH), N_S5_STATE + lo:N_S5_STATE + lo + half]
            nr = ar * hr - ai * hi + br
            ni = ar * hi + ai * hr + bi
            x_ref[pl.ds(r0, BATCH), lo:lo + half] = nr
            x_ref[pl.ds(r0, BATCH), N_S5_STATE + lo:N_S5_STATE + lo + half] = ni
            return nr, ni

        hr, hi = lax.fori_loop(
            0, tt, step,
            (hs_ref[:, lo:lo + half], hs_ref[:, N_S5_STATE + lo:N_S5_STATE + lo + half]),
            unroll=True)
        hs_ref[:, lo:lo + half] = hr
        hs_ref[:, N_S5_STATE + lo:N_S5_STATE + lo + half] = hi

    y = _dot(x_ref[...].astype(BF16), cd_ref[...]) + dsk_ref[...] * u
    y = _gelu(y)
    y = y * _sigmoid(_dot(y.astype(BF16), wglu_ref[...]) + bglu_ref[...])
    y_ref[0] = y[:, 0:LANES]
    y_ref[1] = y[:, LANES:2 * LANES]

    xr = jnp.concatenate([si_ref[2], si_ref[3]], axis=1)
    gate = jnp.concatenate([si_ref[4], si_ref[5]], axis=1)
    hal = (RG_CONV - 1) * BATCH
    xext = jnp.concatenate([halo_ref[...], xr], axis=0)
    halo_ref[...] = xr[rows - hal:rows, :]
    xc = cb_ref[...]
    for k in range(RG_CONV):
        xc = xc + cw_ref[k:k + 1, :] * xext[k * BATCH:k * BATCH + rows, :]
    xcb = xc.astype(BF16)
    r = _sigmoid(_dot(xcb, wa_ref[...]) + ba_ref[...])
    ig = _sigmoid(_dot(xcb, wx_ref[...]) + bx_ref[...])
    nl = -lam_ref[...]
    softplus = jnp.maximum(nl, 0.0) + jnp.log(1.0 + jnp.exp(-jnp.abs(nl)))
    log_a = -RG_C * r * softplus
    a = jnp.exp(log_a)
    mult = jnp.sqrt(1.0 - a * a)
    ab_ref[0] = a
    ab_ref[1] = mult * (ig * xc)

    def rg_step(t, h):
        r0 = pl.multiple_of(t * BATCH, BATCH)
        hn = ab_ref[0, pl.ds(r0, BATCH), :] * h + ab_ref[1, pl.ds(r0, BATCH), :]
        ab_ref[1, pl.ds(r0, BATCH), :] = hn
        return hn

    hrg_ref[...] = lax.fori_loop(0, tt, rg_step, hrg_ref[...], unroll=True)
    yr = ab_ref[1] * _gelu(gate)
    y_ref[2] = yr[:, 0:LANES]
    y_ref[3] = yr[:, LANES:2 * LANES]

    for b in range(BATCH):
        o_ref[b] = jnp.concatenate(
            [y_ref[j, pl.ds(b, tt, stride=BATCH), :] for j in range(4)], axis=1).astype(BF16)


def _scans(scan_in, a8, bd, cd, dsk, wglu, bglu, cw, cb, wa, ba, wx, bx, lam, seq):
    tt = min(seq, 64)
    nt = seq // tt
    rows = tt * BATCH
    params = [a8, bd, cd, dsk, wglu, bglu, cw, cb, wa, ba, wx, bx, lam]
    return pl.pallas_call(
        functools.partial(_scan_kernel, tt),
        out_shape=jax.ShapeDtypeStruct((BATCH, seq, 2 * D_S5), BF16),
        grid=(nt,),
        in_specs=[pl.BlockSpec((6, rows, LANES), lambda i: (0, i, 0))] + [_full_spec(p.shape) for p in params],
        out_specs=pl.BlockSpec((BATCH, tt, 2 * D_S5), lambda i: (0, i, 0)),
        scratch_shapes=[
            pltpu.VMEM((rows, 2 * N_S5_STATE), F32),
            pltpu.VMEM((BATCH, 2 * N_S5_STATE), F32),
            pltpu.VMEM(((RG_CONV - 1) * BATCH, D_RG), F32),
            pltpu.VMEM((2, rows, D_RG), F32),
            pltpu.VMEM((BATCH, D_RG), F32),
            pltpu.VMEM((4, rows, LANES), F32),
        ],
        compiler_params=pltpu.CompilerParams(dimension_semantics=("arbitrary",),
                                             vmem_limit_bytes=VMEM_LIMIT),
        name="scans",
    )(scan_in, *params)


def _count16(ref, skv, cand, op):
    cb = jnp.broadcast_to(cand, (PACK16, LANES))
    accs = []
    for i in range(skv // PACK16):
        one = jnp.where(op(ref[i * PACK16:(i + 1) * PACK16, :], cb), jnp.int16(1), jnp.int16(0))
        if i < 4:
            accs.append(one)
        else:
            accs[i % 4] = accs[i % 4] + one
    tot = (accs[0] + accs[1]) + (accs[2] + accs[3])
    return jnp.sum(tot.astype(I32), axis=0, keepdims=True)


def _transposed_chunks(x):
    xf = x.astype(F32)
    return jnp.concatenate([xf[:, c * LANES:(c + 1) * LANES].T for c in range(4)], axis=1).astype(BF16)


def _attn_kernel(seq, top, q_ref, kz_ref, v_ref, qi_ref, kiz_ref, wi_ref, o_ref,
                 vt_ref, sc_ref, sb_ref, lom_ref, tie_ref, bias_ref, l_ref, p_ref):
    j = pl.program_id(1)

    @pl.when(j == 0)
    def _():
        vt = v_ref[...].astype(F32).T.astype(BF16)
        ones = jnp.where(lax.broadcasted_iota(I32, (PACK16, seq), 0) == 0, 1.0, 0.0).astype(BF16)
        for g in range(N_KV_HEADS):
            vt_ref[g, 0:HEAD_DIM, :] = vt[g * HEAD_DIM:(g + 1) * HEAD_DIM, :]
            vt_ref[g, HEAD_DIM:HEAD_DIM + PACK16, :] = ones

    cls_rows = min(seq, KV_CLASS)
    for c in range(seq // cls_rows):
        @pl.when(j // (cls_rows // LANES) == c)
        def _(c=c):
            skv = cls_rows * (c + 1)
            if skv <= top:
                bias_ref[0:skv, :] = jnp.zeros((skv, LANES), BF16)
                _softmax_pv(skv, j, q_ref, kz_ref, o_ref, vt_ref, bias_ref, l_ref, p_ref)
            else:
                _attn_class(skv, top, j, q_ref, kz_ref, qi_ref, kiz_ref, wi_ref, o_ref,
                            vt_ref, sc_ref, sb_ref, lom_ref, tie_ref, bias_ref, l_ref, p_ref)


def _attn_class(skv, top, j, q_ref, kz_ref, qi_ref, kiz_ref, wi_ref, o_ref,
                vt_ref, sc_ref, sb_ref, lom_ref, tie_ref, bias_ref, l_ref, p_ref):
    ck = ATT_CHUNK
    nc = skv // ck
    mm_chunks = [(r0, min(ATT_MM_CHUNK, skv - r0)) for r0 in range(0, skv, ATT_MM_CHUNK)]
    sub = ATT_SUB
    qb = LANES
    t_idx = j * qb + lax.broadcasted_iota(I32, (ck, qb), 1)
    s_iota = lax.broadcasted_iota(I32, (ck, qb), 0)
    t_sub = j * qb + lax.broadcasted_iota(I32, (sub, qb), 1)
    s_sub = lax.broadcasted_iota(I32, (sub, qb), 0)

    wts = wi_ref[...].T[HEAD_DIM:HEAD_DIM + IDX_HEADS, :] * IDX_SCALE
    qit = _transposed_chunks(qi_ref[...])
    for r0, mk in mm_chunks:
        se = _dot(kiz_ref[r0:r0 + mk, 0:LANES], qit)
        so = _dot(kiz_ref[r0:r0 + mk, LANES:2 * LANES], qit)
        for s0 in range(0, mk, sub):
            acc = None
            for c4 in range(4):
                te = wts[2 * c4:2 * c4 + 1, :] * jnp.maximum(se[s0:s0 + sub, c4 * qb:(c4 + 1) * qb], 0.0)
                to = wts[2 * c4 + 1:2 * c4 + 2, :] * jnp.maximum(so[s0:s0 + sub, c4 * qb:(c4 + 1) * qb], 0.0)
                acc = te + to if acc is None else acc + (te + to)
            rows = slice(r0 + s0, r0 + s0 + sub)
            sc = acc
            if r0 + s0 + sub > skv - KV_CLASS:
                sc = jnp.where((r0 + s0) + s_sub <= t_sub, acc, -jnp.inf)
            sc_ref[rows, :] = sc
            near = sc.astype(BF16)
            rounded_up = (sc - near.astype(F32)).astype(BF16) < 0
            nb = pltpu.bitcast(near, jnp.int16)
            below = pltpu.bitcast(nb + jnp.where(nb < 0, jnp.int16(1), jnp.int16(-1)), BF16)
            sb_ref[rows, :] = jnp.where(rounded_up, below, near)

    ge = lambda a, b: a >= b
    lt = lambda a, b: a < b

    def f32_of_code(u):
        k = u - HALF16
        return pltpu.bitcast(jnp.left_shift(jnp.where(k < 0, k ^ 0x7FFF, k), 16), F32)

    def coarse_body(i, tu):
        cand = jnp.minimum(tu | jnp.left_shift(jnp.int32(1), 15 - i), 65535 - NEG_INF_CODE)
        cnt = _count16(sb_ref, skv, f32_of_code(cand + NEG_INF_CODE).astype(BF16), ge)
        return jnp.where(cnt >= top, cand, tu)

    lo_code = lax.fori_loop(0, 16, coarse_body, jnp.zeros((1, qb), I32)) + NEG_INF_CODE
    lo_f = f32_of_code(lo_code)
    hi_f = f32_of_code(lo_code + 1)
    lo_b = lo_f.astype(BF16)
    hi_b = hi_f.astype(BF16)
    keep, drop = jnp.zeros((), BF16), jnp.full((), -jnp.inf, BF16)
    n_above = _count16(sb_ref, skv, hi_b, ge)
    width = hi_f - lo_f
    usable = jnp.logical_and(width > 1e-30, width < 1e30)
    scale = jnp.where(usable, 65536.0 / jnp.where(usable, width, 1.0), 0.0)
    outside = jnp.int16(-HALF16)

    def residuals(c, carry):
        r0 = pl.multiple_of(c * ck, ck)
        s = sc_ref[pl.ds(r0, ck), :]
        x = jnp.clip(jnp.where(s > lo_f, s - lo_f, 0.0) * scale, 0.0, 65534.0)
        rq = (x.astype(I32) - (HALF16 - 1)).astype(jnp.int16)
        sb = sb_ref[pl.ds(r0, ck), :]
        inside = jnp.logical_and(sb >= lo_b, sb < hi_b)
        lom_ref[pl.ds(r0, ck), :] = jnp.where(inside, rq, outside)
        return carry

    lax.fori_loop(0, nc, residuals, 0)

    def fine_body(i, tu):
        cand = tu | jnp.left_shift(jnp.int32(1), 15 - i)
        cnt = _count16(lom_ref, skv, (cand - HALF16).astype(jnp.int16), ge)
        return jnp.where(cnt >= top - n_above, cand, tu)

    cut = (lax.fori_loop(0, 16, fine_body, jnp.zeros((1, qb), I32)) - HALF16).astype(jnp.int16)

    def bias_default(c, acc):
        r0 = pl.multiple_of(c * ck, ck)
        at_cut = lom_ref[pl.ds(r0, ck), :] >= cut
        sel = jnp.logical_or(sb_ref[pl.ds(r0, ck), :] >= hi_b, at_cut)
        bias_ref[pl.ds(r0, ck), :] = jnp.where(sel, keep, drop)
        ones = jnp.where(at_cut, jnp.int16(1), jnp.int16(0))
        parts = [ones[i * PACK16:(i + 1) * PACK16, :] for i in range(ck // PACK16)]
        while len(parts) > 1:
            parts = [a + b for a, b in zip(parts[0::2], parts[1::2])]
        return acc + parts[0]

    n_cut = lax.fori_loop(0, nc, bias_default, jnp.zeros((PACK16, qb), jnp.int16))
    n_ge = n_above + jnp.sum(n_cut.astype(I32), axis=0, keepdims=True)
    tie = jnp.logical_and(n_ge > top, lo_code != NEG_INF_CODE)
    any_tie = jnp.max(jnp.where(tie, 1, 0).astype(I32)) > 0

    @pl.when(any_tie)
    def _tie():
        nbits = (skv - 1).bit_length()
        n_over = _count16(lom_ref, skv, cut, lambda a, b: a > b)
        need = top - n_above - n_over
        big = jnp.int16(2 ** nbits)

        def fill(c, carry):
            r0 = pl.multiple_of(c * ck, ck)
            idx = (r0 + s_iota).astype(jnp.int16)
            tie_ref[pl.ds(r0, ck), :] = jnp.where(lom_ref[pl.ds(r0, ck), :] == cut, idx, big)
            return carry

        lax.fori_loop(0, nc, fill, 0)

        def bit2(i, m):
            cand = m | jnp.left_shift(jnp.int32(1), nbits - 1 - i)
            cnt = _count16(tie_ref, skv, cand.astype(jnp.int16), lt)
            return jnp.where(cnt < need, cand, m)

        m = lax.fori_loop(0, nbits, bit2, jnp.zeros((1, qb), I32)).astype(jnp.int16)

        def bias_tie(c, carry):
            r0 = pl.multiple_of(c * ck, ck)
            sel = jnp.logical_or(sb_ref[pl.ds(r0, ck), :] >= hi_b,
                                 lom_ref[pl.ds(r0, ck), :] > cut)
            sel = jnp.logical_or(sel, tie_ref[pl.ds(r0, ck), :] <= m)
            bias_ref[pl.ds(r0, ck), :] = jnp.where(sel, keep, drop)
            return carry

        lax.fori_loop(0, nc, bias_tie, 0)

    _softmax_pv(skv, j, q_ref, kz_ref, o_ref, vt_ref, bias_ref, l_ref, p_ref)


def _softmax_pv(skv, j, q_ref, kz_ref, o_ref, vt_ref, bias_ref, l_ref, p_ref):
    ck = ATT_CHUNK
    nc = skv // ck
    mm_chunks = [(r0, min(ATT_MM_CHUNK, skv - r0)) for r0 in range(0, skv, ATT_MM_CHUNK)]
    sub = ATT_SUB
    qb = LANES
    t_sub = j * qb + lax.broadcasted_iota(I32, (sub, qb), 1)
    s_sub = lax.broadcasted_iota(I32, (sub, qb), 0)
    qt = _transposed_chunks(q_ref[...])
    o_parts = []
    for g in range(N_KV_HEADS):
        qet = qt[:, (2 * g) * qb:(2 * g + 2) * qb]
        m8 = jnp.full((SUBLANES, 4 * qb), -jnp.inf, F32)
        for r0, mk in mm_chunks:
            le = _dot(kz_ref[r0:r0 + mk, (2 * g) * LANES:(2 * g + 1) * LANES], qet)
            lo = _dot(kz_ref[r0:r0 + mk, (2 * g + 1) * LANES:(2 * g + 2) * LANES], qet)
            for s0 in range(0, mk, sub):
                rows = slice(r0 + s0, r0 + s0 + sub)
                b = bias_ref[rows, :].astype(F32)
                if r0 + s0 + sub > skv - KV_CLASS:
                    b = jnp.where((r0 + s0) + s_sub <= t_sub, b, -jnp.inf)
                l = jnp.concatenate([le[s0:s0 + sub] + jnp.concatenate([b, b], axis=1),
                                     lo[s0:s0 + sub] + jnp.concatenate([b, b], axis=1)], axis=1)
                l_ref[rows, :] = l
                m8 = jnp.maximum(m8, jnp.max(l.reshape(sub // SUBLANES, SUBLANES, 4 * qb), axis=0))
        mx = jnp.max(m8, axis=0, keepdims=True)

        def pbody(c, carry, mx=mx):
            r0 = pl.multiple_of(c * ck, ck)
            p_ref[pl.ds(r0, ck), :] = jnp.exp2(l_ref[pl.ds(r0, ck), :] - mx).astype(BF16)
            return carry

        lax.fori_loop(0, nc, pbody, 0)
        ot = _dot(vt_ref[g, :, 0:skv], p_ref[0:skv, :])
        o_parts.append(ot[0:HEAD_DIM, :] * (1.0 / ot[HEAD_DIM:HEAD_DIM + 1, :]))
    ot = jnp.concatenate(o_parts, axis=0)
    y = jnp.concatenate([ot[:, i * qb:(i + 1) * qb].T for i in range(4)], axis=1)
    o_ref[...] = y.astype(BF16)


def _attention(q, kz, v, qi, kiz, wi, seq, top):
    qb = LANES
    sq = pl.Squeezed()
    per_q = lambda c: pl.BlockSpec((sq, qb, c), lambda b, j: (b, j, 0))
    per_b = lambda c: pl.BlockSpec((sq, seq, c), lambda b, j: (b, 0, 0))
    return pl.pallas_call(
        functools.partial(_attn_kernel, seq, top),
        out_shape=jax.ShapeDtypeStruct((BATCH, seq, D_ATTN), BF16),
        grid=(BATCH, seq // qb),
        in_specs=[per_q(D_ATTN), per_b(4 * LANES), per_b(KV_DIM), per_q(512), per_b(2 * LANES), per_q(LANES)],
        out_specs=per_q(D_ATTN),
        scratch_shapes=[
            pltpu.VMEM((N_KV_HEADS, HEAD_DIM + PACK16, seq), BF16),
            pltpu.VMEM((seq, qb), F32),
            pltpu.VMEM((seq, qb), BF16),
            pltpu.VMEM((seq, qb), jnp.int16),
            pltpu.VMEM((seq, qb), jnp.int16),
            pltpu.VMEM((seq, qb), BF16),
            pltpu.VMEM((seq, 4 * qb), F32),
            pltpu.VMEM((seq, 4 * qb), BF16),
        ],
        compiler_params=pltpu.CompilerParams(dimension_semantics=("parallel", "arbitrary"),
                                             vmem_limit_bytes=VMEM_LIMIT),
        name="dsa_attention",
    )(q, kz, v, qi, kiz, wi)


def _mix_ffn_kernel(ys_ref, ya_ref, h_ref, p_ref, wo_ref, g1_ref, b1_ref,
                    wup_ref, wd_ref, wpg_ref, wpp_ref, g2_ref, b2_ref, o_ref):
    mix = _dot(ys_ref[...], wo_ref[0:2 * D_S5, :]) + _dot(ya_ref[...], wo_ref[2 * D_S5:, :])
    h = _ln(ALPHA * h_ref[...] + mix, g1_ref[...], b1_ref[...])
    hb = h.astype(BF16)
    gate = _dot(hb, wup_ref[:, 0:D_FF])
    up = _dot(hb, wup_ref[:, D_FF:])
    act = (gate * _sigmoid(gate) * up).astype(BF16)
    ffn = _dot(act, wd_ref[...])
    ple = _sigmoid(_dot(hb, wpg_ref[...])) * _dot(p_ref[...].astype(BF16), wpp_ref[...])
    o_ref[...] = _ln(ALPHA * h + ffn + ple, g2_ref[...], b2_ref[...])


def _mix_ffn(layer, y_scan, y_at, h, p, w_out_p, g1, b1, w_up, w_down, w_pg, w_pp, g2, b2):
    t = h.shape[0]
    tm = min(t, 512)
    row = lambda c: pl.BlockSpec((tm, c), lambda i: (i, 0))
    const = lambda shape: pl.BlockSpec(shape, lambda i: (0, 0), pipeline_mode=pl.Buffered(1))
    of_layer = lambda r, c: pl.BlockSpec((pl.Squeezed(), r, c), lambda i: (layer, 0, 0),
                                         pipeline_mode=pl.Buffered(1))
    vec = _full_spec((1, D_MODEL))
    return pl.pallas_call(
        _mix_ffn_kernel,
        out_shape=jax.ShapeDtypeStruct((t, D_MODEL), F32),
        grid=(t // tm,),
        in_specs=[row(2 * D_S5), row(D_ATTN), row(D_MODEL), pl.BlockSpec((pl.Squeezed(), tm, D_PLE), lambda i: (layer, i, 0)),
                  const((D_MIX, D_MODEL)), vec, vec,
                  of_layer(D_MODEL, 2 * D_FF), of_layer(D_FF, D_MODEL), of_layer(D_MODEL, D_MODEL),
                  of_layer(D_PLE, D_MODEL), vec, vec],
        out_specs=row(D_MODEL),
        compiler_params=pltpu.CompilerParams(dimension_semantics=("parallel",),
                                             vmem_limit_bytes=VMEM_LIMIT),
        name="mix_ffn_ln",
    )(y_scan, y_at, h, p, w_out_p, g1, b1, w_up, w_down, w_pg, w_pp, g2, b2)


def _block_diag(w):
    hh, n, m = w.shape
    eye = jnp.eye(hh, dtype=w.dtype)
    return jnp.einsum('hij,hk->hikj', w, eye).reshape(hh * n, hh * m)


def _s5_out_matrix(c_re, c_im):
    cre = _block_diag(c_re.transpose(0, 2, 1))
    cim = _block_diag(c_im.transpose(0, 2, 1))
    return jnp.concatenate([cre, -cim], axis=0).astype(BF16)


def _attn_out_rows(w_at):
    w = w_at.reshape(N_KV_HEADS, 4, HEAD_DIM, D_MODEL)
    w = w[:, jnp.array([0, 2, 1, 3])]
    return w.transpose(1, 0, 2, 3).reshape(D_ATTN, D_MODEL)


def _forward(seq, top, x, p, positions, ln_emb_g, ln_emb_b, w_in,
             s5_lam_re, s5_lam_im, s5_log_step, s5_b_re, s5_b_im, s5_c_re, s5_c_im,
             s5_d, s5_w_glu, s5_b_glu,
             rg_conv_w, rg_conv_b, rg_wa, rg_ba, rg_wx, rg_bx, rg_lam,
             w_out, ln1_g, ln1_b, ffn_w_up, ffn_w_down, ple_w_gate, ple_w_proj,
             ln2_g, ln2_b):
    t = BATCH * seq
    cos, sin = _rope_tables(positions, seq)
    row = lambda a: a.reshape(1, -1)
    p_rows = p.reshape(DEPTH, t, D_PLE)
    w_up_b, w_down_b = ffn_w_up.astype(BF16), ffn_w_down.astype(BF16)
    w_pg_b, w_pp_b = ple_w_gate.astype(BF16), ple_w_proj.astype(BF16)
    h3 = x
    for i in range(DEPTH):
        w_in_p = jnp.pad(w_in[i], ((0, 0), (0, N_IN_PAD - N_IN))).astype(BF16)
        outs = _inproj(h3, w_in_p, cos, sin, row(ln_emb_g), row(ln_emb_b), i == 0, seq)
        if i == 0:
            h3, outs = outs[0], outs[1:]
        scan_in, q, kz, v, qi, kiz, wi = outs
        a8, bd = _s5_params(s5_lam_re[i], s5_lam_im[i], s5_log_step[i], s5_b_re[i], s5_b_im[i])
        y_scan = _scans(scan_in, a8, bd, _s5_out_matrix(s5_c_re[i], s5_c_im[i]), row(s5_d[i]),
                        s5_w_glu[i].astype(BF16), row(s5_b_glu[i]),
                        rg_conv_w[i], row(rg_conv_b[i]),
                        _block_diag(rg_wa[i]).astype(BF16), row(rg_ba[i]),
                        _block_diag(rg_wx[i]).astype(BF16), row(rg_bx[i]), row(rg_lam[i]), seq)
        y_at = _attention(q, kz, v, qi, kiz, wi, seq, top)
        w_out_p = jnp.concatenate([w_out[i][:2 * D_S5], _attn_out_rows(w_out[i][2 * D_S5:])], axis=0).astype(BF16)
        h2 = _mix_ffn(i, y_scan.reshape(t, 2 * D_S5), y_at.reshape(t, D_ATTN), h3.reshape(t, D_MODEL),
                      p_rows, w_out_p, row(ln1_g[i]), row(ln1_b[i]),
                      w_up_b, w_down_b, w_pg_b, w_pp_b, row(ln2_g[i]), row(ln2_b[i]))
        h3 = h2.reshape(BATCH, seq, D_MODEL)
    return h3


def kernel(x, p, positions, ln_emb_g, ln_emb_b, w_in, s5_lam_re, s5_lam_im, s5_log_step, s5_b_re, s5_b_im, s5_c_re, s5_c_im, s5_d, s5_w_glu, s5_b_glu, rg_conv_w, rg_conv_b, rg_wa, rg_ba, rg_wx, rg_bx, rg_lam, w_out, ln1_g, ln1_b, ffn_w_up, ffn_w_down, ple_w_gate, ple_w_proj, ln2_g, ln2_b):
    seq = x.shape[1]
    return _forward(seq, min(TOPK_MAX, seq // 4), x, p, positions, ln_emb_g, ln_emb_b, w_in,
                    s5_lam_re, s5_lam_im, s5_log_step, s5_b_re, s5_b_im, s5_c_re, s5_c_im,
                    s5_d, s5_w_glu, s5_b_glu,
                    rg_conv_w, rg_conv_b, rg_wa, rg_ba, rg_wx, rg_bx, rg_lam,
                    w_out, ln1_g, ln1_b, ffn_w_up, ffn_w_down, ple_w_gate, ple_w_proj,
                    ln2_g, ln2_b)
```

```python
import functools
import math

import jax
import jax.numpy as jnp
from jax import lax
from jax.experimental import pallas as pl
from jax.experimental.pallas import tpu as pltpu

F32 = jnp.float32
BF16 = jnp.bfloat16
I32 = jnp.int32

D_MODEL = 1024
BATCH = 8
SEQ = 2048
DEPTH = 2
D_S5 = 256
S5_GROUP = 16
S5_GROUPS = 16
S5_STATE = 64
N_S5_STATE = S5_GROUPS * S5_STATE
D_RG = 256
RG_BLOCKS = 8
RG_BLOCK = 32
RG_CONV = 4
RG_C = 8.0
N_HEADS = 8
N_KV_HEADS = 2
HEAD_DIM = 64
D_ATTN = 512
KV_DIM = 128
IDX_HEADS = 8
IDX_DIM = 64
TOPK_MAX = 256
D_MIX = 1024
ROPE_THETA = 10000.0
D_FF = 2816
D_PLE = 256
ALPHA = (2.0 * DEPTH) ** 0.25
LN_EPS = 1e-5
N_IN = 2120
N_IN_PAD = 2176
ATT_SCALE = HEAD_DIM ** -0.5
IDX_SCALE = (IDX_HEADS * IDX_DIM) ** -0.5
LOG2E = math.log2(math.e)

LANES = 128
SUBLANES = 8
PACK16 = 16
INPROJ_BATCHES = 4
HALF16 = 32768
KV_CLASS = 256
ATT_CHUNK = 256
ATT_MM_CHUNK = 512
ATT_SUB = 64
NEG_INF_CODE = 127

VMEM_LIMIT = 56 * 1024 * 1024


def _dot(a, b):
    return jnp.dot(a, b, preferred_element_type=F32)


def _dot_nt(a, b):
    return lax.dot_general(a, b, (((1,), (1,)), ((), ())), preferred_element_type=F32)


def _ln(x, g, b):
    mu = jnp.mean(x, axis=-1, keepdims=True)
    xc = x - mu
    var = jnp.mean(xc * xc, axis=-1, keepdims=True)
    return xc * lax.rsqrt(var + LN_EPS) * g + b


def _gelu(x):
    c = math.sqrt(2.0 / math.pi)
    return 0.5 * x * (1.0 + jnp.tanh(c * (x + 0.044715 * (x * x * x))))


def _sigmoid(x):
    return 1.0 / (1.0 + jnp.exp(-x))


def _full_spec(shape):
    n = len(shape)
    return pl.BlockSpec(shape, lambda *_: (0,) * n)


def _rope_table_kernel(pos_ref, inv_ref, sgn_ref, cos_ref, sin_ref):
    ang = pos_ref[...].astype(F32) * inv_ref[...]
    cos_ref[...] = jnp.cos(ang)
    sin_ref[...] = jnp.sin(ang) * sgn_ref[...]


def _rope_tables(positions, seq):
    t = BATCH * seq
    tm = min(t, 2048)
    inv = ROPE_THETA ** (-jnp.arange(0, HEAD_DIM, 2, dtype=F32) / HEAD_DIM)
    inv128 = jnp.tile(inv, 4)[None, :]
    sgn = jnp.where((jnp.arange(LANES) % HEAD_DIM) < HEAD_DIM // 2, -1.0, 1.0).astype(F32)[None, :]
    pos = positions.reshape(t, 1)
    cos, sin = pl.pallas_call(
        _rope_table_kernel,
        out_shape=(jax.ShapeDtypeStruct((t, LANES), F32),) * 2,
        grid=(t // tm,),
        in_specs=[pl.BlockSpec((tm, 1), lambda i: (i, 0)), _full_spec((1, LANES)), _full_spec((1, LANES))],
        out_specs=(pl.BlockSpec((tm, LANES), lambda i: (i, 0)),) * 2,
        compiler_params=pltpu.CompilerParams(dimension_semantics=("parallel",)),
        name="rope_tables",
    )(pos, inv128, sgn)
    return cos.reshape(BATCH, seq, LANES), sin.reshape(BATCH, seq, LANES)


def _inproj_kernel(apply_ln, tt, h_ref, w_ref, cos_ref, sin_ref, g_ref, b_ref, *outs):
    nb = INPROJ_BATCHES
    for b0 in range(0, BATCH, nb):
        _inproj_rows(apply_ln, tt, b0, nb, h_ref, w_ref, cos_ref, sin_ref, g_ref, b_ref, outs)


def _inproj_rows(apply_ln, tt, b0, nb, h_ref, w_ref, cos_ref, sin_ref, g_ref, b_ref, outs):
    if apply_ln:
        hn_ref, so_ref, q_ref, kz_ref, v_ref, qi_ref, kiz_ref, wi_ref = outs
    else:
        so_ref, q_ref, kz_ref, v_ref, qi_ref, kiz_ref, wi_ref = outs
    bs = slice(b0, b0 + nb)
    rows = nb * tt
    h = h_ref[bs].reshape(rows, D_MODEL)
    if apply_ln:
        h = _ln(h, g_ref[...], b_ref[...])
        hn_ref[bs] = h.reshape(nb, tt, D_MODEL)
    hb = h.astype(BF16)
    cos = cos_ref[bs].reshape(rows, LANES)
    sin = sin_ref[bs].reshape(rows, LANES)
    lane = lax.broadcasted_iota(I32, (rows, LANES), 1)
    first_half = (lane & (HEAD_DIM // 2)) == 0
    lo64 = lane < HEAD_DIM

    def rope(x, cs, sn):
        partner = jnp.where(first_half, pltpu.roll(x, LANES - 32, 1), pltpu.roll(x, 32, 1))
        return x * cs + partner * sn

    def rope_wide(x, n_chunks):
        return jnp.concatenate(
            [rope(x[:, c * LANES:(c + 1) * LANES], cos, sin) for c in range(n_chunks)], axis=1)

    ps = _dot(hb, w_ref[:, 0:768])
    for j in range(6):
        for b in range(nb):
            so_ref[j, pl.ds(b0 + b, tt, stride=BATCH), :] = ps[b * tt:(b + 1) * tt, j * LANES:(j + 1) * LANES]

    pq = _dot(hb, w_ref[:, 768:1280])
    q_ref[bs] = (rope_wide(pq, 4) * (ATT_SCALE * LOG2E)).astype(BF16).reshape(nb, tt, D_ATTN)

    pkv = _dot(hb, w_ref[:, 1280:1536])
    kr = rope(pkv[:, 0:LANES], cos, sin)
    ksw = pltpu.roll(kr, HEAD_DIM, 1)
    zero = jnp.zeros_like(kr)
    kz = jnp.concatenate([jnp.where(lo64, kr, zero), jnp.where(lo64, zero, ksw),
                          jnp.where(lo64, ksw, zero), jnp.where(lo64, zero, kr)], axis=1)
    kz_ref[bs] = kz.astype(BF16).reshape(nb, tt, 4 * LANES)
    v_ref[bs] = pkv[:, LANES:2 * LANES].astype(BF16).reshape(nb, tt, KV_DIM)

    pqi = _dot(hb, w_ref[:, 1536:2048])
    qi_ref[bs] = rope_wide(pqi, 4).astype(BF16).reshape(nb, tt, IDX_HEADS * IDX_DIM)

    pk = _dot(hb, w_ref[:, 2048:N_IN_PAD])
    kir = rope(pk, jnp.where(lo64, cos, 1.0), jnp.where(lo64, sin, 0.0))
    kie = jnp.where(lo64, kir, zero)
    kiz = jnp.concatenate([kie, pltpu.roll(kie, HEAD_DIM, 1)], axis=1)
    kiz_ref[bs] = kiz.astype(BF16).reshape(nb, tt, 2 * LANES)
    wi_ref[bs] = pk.reshape(nb, tt, LANES)


def _inproj(h3, w_in_p, cos, sin, ln_g, ln_b, apply_ln, seq):
    tt = min(seq, 128)
    nt = seq // tt
    blk = lambda c: pl.BlockSpec((BATCH, tt, c), lambda i: (0, i, 0))
    out_shape = [
        jax.ShapeDtypeStruct((6, seq * BATCH, LANES), F32),
        jax.ShapeDtypeStruct((BATCH, seq, D_ATTN), BF16),
        jax.ShapeDtypeStruct((BATCH, seq, 4 * LANES), BF16),
        jax.ShapeDtypeStruct((BATCH, seq, KV_DIM), BF16),
        jax.ShapeDtypeStruct((BATCH, seq, IDX_HEADS * IDX_DIM), BF16),
        jax.ShapeDtypeStruct((BATCH, seq, 2 * LANES), BF16),
        jax.ShapeDtypeStruct((BATCH, seq, LANES), F32),
    ]
    out_specs = [pl.BlockSpec((6, tt * BATCH, LANES), lambda i: (0, i, 0)),
                 blk(D_ATTN), blk(4 * LANES), blk(KV_DIM), blk(512), blk(2 * LANES), blk(LANES)]
    if apply_ln:
        out_shape = [jax.ShapeDtypeStruct((BATCH, seq, D_MODEL), F32)] + out_shape
        out_specs = [blk(D_MODEL)] + out_specs
    return pl.pallas_call(
        functools.partial(_inproj_kernel, apply_ln, tt),
        out_shape=tuple(out_shape),
        grid=(nt,),
        in_specs=[blk(D_MODEL), _full_spec((D_MODEL, N_IN_PAD)), blk(LANES), blk(LANES),
                  _full_spec((1, D_MODEL)), _full_spec((1, D_MODEL))],
        out_specs=tuple(out_specs),
        compiler_params=pltpu.CompilerParams(dimension_semantics=("parallel",),
                                             vmem_limit_bytes=VMEM_LIMIT),
        name="inproj_ln" if apply_ln else "inproj",
    )(h3, w_in_p, cos, sin, ln_g, ln_b)


def _s5_param_kernel(lr_ref, li_ref, ls_ref, br_ref, bi_ref, a_ref, bd_ref):
    lr = lr_ref[...]
    li = li_ref[...]
    step = jnp.exp(ls_ref[...])
    mag = jnp.exp(lr * step)
    ar = mag * jnp.cos(li * step)
    ai = mag * jnp.sin(li * step)
    den = lr * lr + li * li
    nr, ni = ar - 1.0, ai
    cr = (nr * lr + ni * li) / den
    ci = (ni * lr - nr * li) / den
    br = br_ref[...]
    bi = bi_ref[...]
    bbr = cr * br - ci * bi
    bbi = cr * bi + ci * br
    row = lax.broadcasted_iota(I32, (D_S5, N_S5_STATE), 0)
    col = lax.broadcasted_iota(I32, (D_S5, N_S5_STATE), 1)
    blk = (row // S5_GROUP) == (col // S5_STATE)
    zero = jnp.zeros((D_S5, N_S5_STATE), F32)
    bd_ref[:, 0:N_S5_STATE] = jnp.where(blk, jnp.concatenate([bbr] * S5_GROUPS, axis=0), zero).astype(BF16)
    bd_ref[:, N_S5_STATE:] = jnp.where(blk, jnp.concatenate([bbi] * S5_GROUPS, axis=0), zero).astype(BF16)
    a_ref[:, 0:N_S5_STATE] = jnp.broadcast_to(ar, (SUBLANES, N_S5_STATE))
    a_ref[:, N_S5_STATE:] = jnp.broadcast_to(ai, (SUBLANES, N_S5_STATE))


def _s5_params(lam_re, lam_im, log_step, b_re, b_im):
    lr = lam_re.reshape(1, N_S5_STATE)
    li = lam_im.reshape(1, N_S5_STATE)
    ls = jnp.repeat(log_step, S5_STATE).reshape(1, N_S5_STATE)
    br = b_re.transpose(2, 0, 1).reshape(S5_GROUP, N_S5_STATE)
    bi = b_im.transpose(2, 0, 1).reshape(S5_GROUP, N_S5_STATE)
    return pl.pallas_call(
        _s5_param_kernel,
        out_shape=(jax.ShapeDtypeStruct((SUBLANES, 2 * N_S5_STATE), F32),
                   jax.ShapeDtypeStruct((D_S5, 2 * N_S5_STATE), BF16)),
        name="s5_params",
    )(lr, li, ls, br, bi)


def _scan_kernel(tt, si_ref, a_ref, bd_ref, cd_ref, dsk_ref, wglu_ref, bglu_ref,
                 cw_ref, cb_ref, wa_ref, ba_ref, wx_ref, bx_ref, lam_ref,
                 o_ref, x_ref, hs_ref, halo_ref, ab_ref, hrg_ref, y_ref):
    rows = BATCH * tt
    i = pl.program_id(0)

    @pl.when(i == 0)
    def _init():
        hs_ref[...] = jnp.zeros_like(hs_ref)
        halo_ref[...] = jnp.zeros_like(halo_ref)
        hrg_ref[...] = jnp.zeros_like(hrg_ref)

    u = jnp.concatenate([si_ref[0], si_ref[1]], axis=1)
    x_ref[...] = _dot(u.astype(BF16), bd_ref[...])
    half = N_S5_STATE // 2
    for hh in range(2):
        lo = hh * half
        ar = a_ref[:, lo:lo + half]
        ai = a_ref[:, N_S5_STATE + lo:N_S5_STATE + lo + half]

        def step(t, carry, lo=lo, ar=ar, ai=ai):
            hr, hi = carry
            r0 = pl.multiple_of(t * BATCH, BATCH)
            br = x_ref[pl.ds(r0, BATCH), lo:lo + half]
            bi = x_ref[pl.ds(r0, BATCH), N_S5_STATE + lo:N_S5_STATE + lo + half]
            nr = ar * hr - ai * hi + br
            ni = ar * hi + ai * hr + bi
            x_ref[pl.ds(r0, BATCH), lo:lo + half] = nr
            x_ref[pl.ds(r0, BATCH), N_S5_STATE + lo:N_S5_STATE + lo + half] = ni
            return nr, ni

        hr, hi = lax.fori_loop(
            0, tt, step,
            (hs_ref[:, lo:lo + half], hs_ref[:, N_S5_STATE + lo:N_S5_STATE + lo + half]),
            unroll=True)
        hs_ref[:, lo:lo + half] = hr
        hs_ref[:, N_S5_STATE + lo:N_S5_STATE + lo + half] = hi

    y = _dot(x_ref[...].astype(BF16), cd_ref[...]) + dsk_ref[...] * u
    y = _gelu(y)
    y = y * _sigmoid(_dot(y.astype(BF16), wglu_ref[...]) + bglu_ref[...])
    y_ref[0] = y[:, 0:LANES]
    y_ref[1] = y[:, LANES:2 * LANES]

    xr = jnp.concatenate([si_ref[2], si_ref[3]], axis=1)
    gate = jnp.concatenate([si_ref[4], si_ref[5]], axis=1)
    hal = (RG_CONV - 1) * BATCH
    xext = jnp.concatenate([halo_ref[...], xr], axis=0)
    halo_ref[...] = xr[rows - hal:rows, :]
    xc = cb_ref[...]
    for k in range(RG_CONV):
        xc = xc + cw_ref[k:k + 1, :] * xext[k * BATCH:k * BATCH + rows, :]
    xcb = xc.astype(BF16)
    r = _sigmoid(_dot(xcb, wa_ref[...]) + ba_ref[...])
    ig = _sigmoid(_dot(xcb, wx_ref[...]) + bx_ref[...])
    nl = -lam_ref[...]
    softplus = jnp.maximum(nl, 0.0) + jnp.log(1.0 + jnp.exp(-jnp.abs(nl)))
    log_a = -RG_C * r * softplus
    a = jnp.exp(log_a)
    mult = jnp.sqrt(1.0 - a * a)
    ab_ref[0] = a
    ab_ref[1] = mult * (ig * xc)

    def rg_step(t, h):
        r0 = pl.multiple_of(t * BATCH, BATCH)
        hn = ab_ref[0, pl.ds(r0, BATCH), :] * h + ab_ref[1, pl.ds(r0, BATCH), :]
        ab_ref[1, pl.ds(r0, BATCH), :] = hn
        return hn

    hrg_ref[...] = lax.fori_loop(0, tt, rg_step, hrg_ref[...], unroll=True)
    yr = ab_ref[1] * _gelu(gate)
    y_ref[2] = yr[:, 0:LANES]
    y_ref[3] = yr[:, LANES:2 * LANES]

    for b in range(BATCH):
        o_ref[b] = jnp.concatenate(
            [y_ref[j, pl.ds(b, tt, stride=BATCH), :] for j in range(4)], axis=1).astype(BF16)


def _scans(scan_in, a8, bd, cd, dsk, wglu, bglu, cw, cb, wa, ba, wx, bx, lam, seq):
    tt = min(seq, 64)
    nt = seq // tt
    rows = tt * BATCH
    params = [a8, bd, cd, dsk, wglu, bglu, cw, cb, wa, ba, wx, bx, lam]
    return pl.pallas_call(
        functools.partial(_scan_kernel, tt),
        out_shape=jax.ShapeDtypeStruct((BATCH, seq, 2 * D_S5), BF16),
        grid=(nt,),
        in_specs=[pl.BlockSpec((6, rows, LANES), lambda i: (0, i, 0))] + [_full_spec(p.shape) for p in params],
        out_specs=pl.BlockSpec((BATCH, tt, 2 * D_S5), lambda i: (0, i, 0)),
        scratch_shapes=[
            pltpu.VMEM((rows, 2 * N_S5_STATE), F32),
            pltpu.VMEM((BATCH, 2 * N_S5_STATE), F32),
            pltpu.VMEM(((RG_CONV - 1) * BATCH, D_RG), F32),
            pltpu.VMEM((2, rows, D_RG), F32),
            pltpu.VMEM((BATCH, D_RG), F32),
            pltpu.VMEM((4, rows, LANES), F32),
        ],
        compiler_params=pltpu.CompilerParams(dimension_semantics=("arbitrary",),
                                             vmem_limit_bytes=VMEM_LIMIT),
        name="scans",
    )(scan_in, *params)


def _count16(ref, skv, cand, op):
    cb = jnp.broadcast_to(cand, (PACK16, LANES))
    accs = []
    for i in range(skv // PACK16):
        one = jnp.where(op(ref[i * PACK16:(i + 1) * PACK16, :], cb), jnp.int16(1), jnp.int16(0))
        if i < 4:
            accs.append(one)
        else:
            accs[i % 4] = accs[i % 4] + one
    tot = (accs[0] + accs[1]) + (accs[2] + accs[3])
    return jnp.sum(tot.astype(I32), axis=0, keepdims=True)


def _transposed_chunks(x):
    xf = x.astype(F32)
    return jnp.concatenate([xf[:, c * LANES:(c + 1) * LANES].T for c in range(4)], axis=1).astype(BF16)


def _attn_kernel(seq, top, q_ref, kz_ref, v_ref, qi_ref, kiz_ref, wi_ref, o_ref,
                 vt_ref, sc_ref, sb_ref, lom_ref, tie_ref, bias_ref, l_ref, p_ref):
    step = pl.program_id(1)

    @pl.when(step == 0)
    def _():
        vt = v_ref[...].astype(F32).T.astype(BF16)
        ones = jnp.where(lax.broadcasted_iota(I32, (PACK16, seq), 0) == 0, 1.0, 0.0).astype(BF16)
        for g in range(N_KV_HEADS):
            vt_ref[g, 0:HEAD_DIM, :] = vt[g * HEAD_DIM:(g + 1) * HEAD_DIM, :]
            vt_ref[g, HEAD_DIM:HEAD_DIM + PACK16, :] = ones

    cls_rows = min(seq, KV_CLASS)
    blocks = cls_rows // LANES
    for c in range(seq // cls_rows):
        @pl.when(step == c)
        def _(c=c):
            skv = cls_rows * (c + 1)

            def one_block(hb, carry):
                j = step * blocks + hb
                rows = pl.ds(pl.multiple_of(hb * LANES, LANES), LANES)
                qv, qiv, wiv, ov = q_ref.at[rows], qi_ref.at[rows], wi_ref.at[rows], o_ref.at[rows]
                if skv <= top:
                    bias_ref[0:skv, :] = jnp.zeros((skv, LANES), BF16)
                    _softmax_pv(skv, j, qv, kz_ref, ov, vt_ref, bias_ref, l_ref, p_ref)
                else:
                    _attn_class(skv, top, j, qv, kz_ref, qiv, kiz_ref, wiv, ov,
                                vt_ref, sc_ref, sb_ref, lom_ref, tie_ref, bias_ref, l_ref, p_ref)
                return carry

            lax.fori_loop(0, blocks, one_block, 0)


def _attn_class(skv, top, j, q_ref, kz_ref, qi_ref, kiz_ref, wi_ref, o_ref,
                vt_ref, sc_ref, sb_ref, lom_ref, tie_ref, bias_ref, l_ref, p_ref):
    ck = ATT_CHUNK
    nc = skv // ck
    mm_chunks = [(r0, min(ATT_MM_CHUNK, skv - r0)) for r0 in range(0, skv, ATT_MM_CHUNK)]
    sub = ATT_SUB
    qb = LANES
    t_idx = j * qb + lax.broadcasted_iota(I32, (ck, qb), 1)
    s_iota = lax.broadcasted_iota(I32, (ck, qb), 0)
    t_sub = j * qb + lax.broadcasted_iota(I32, (sub, qb), 1)
    s_sub = lax.broadcasted_iota(I32, (sub, qb), 0)

    wts = wi_ref[...].T[HEAD_DIM:HEAD_DIM + IDX_HEADS, :] * IDX_SCALE
    qit = _transposed_chunks(qi_ref[...])
    for r0, mk in mm_chunks:
        se = _dot(kiz_ref[r0:r0 + mk, 0:LANES], qit)
        so = _dot(kiz_ref[r0:r0 + mk, LANES:2 * LANES], qit)
        for s0 in range(0, mk, sub):
            acc = None
            for c4 in range(4):
                te = wts[2 * c4:2 * c4 + 1, :] * jnp.maximum(se[s0:s0 + sub, c4 * qb:(c4 + 1) * qb], 0.0)
                to = wts[2 * c4 + 1:2 * c4 + 2, :] * jnp.maximum(so[s0:s0 + sub, c4 * qb:(c4 + 1) * qb], 0.0)
                acc = te + to if acc is None else acc + (te + to)
            rows = slice(r0 + s0, r0 + s0 + sub)
            sc = acc
            if r0 + s0 + sub > skv - KV_CLASS:
                sc = jnp.where((r0 + s0) + s_sub <= t_sub, acc, -jnp.inf)
            sc_ref[rows, :] = sc
            near = sc.astype(BF16)
            rounded_up = (sc - near.astype(F32)).astype(BF16) < 0
            nb = pltpu.bitcast(near, jnp.int16)
            below = pltpu.bitcast(nb + jnp.where(nb < 0, jnp.int16(1), jnp.int16(-1)), BF16)
            sb_ref[rows, :] = jnp.where(rounded_up, below, near)

    ge = lambda a, b: a >= b
    lt = lambda a, b: a < b

    def f32_of_code(u):
        k = u - HALF16
        return pltpu.bitcast(jnp.left_shift(jnp.where(k < 0, k ^ 0x7FFF, k), 16), F32)

    def coarse_body(i, tu):
        cand = jnp.minimum(tu | jnp.left_shift(jnp.int32(1), 15 - i), 65535 - NEG_INF_CODE)
        cnt = _count16(sb_ref, skv, f32_of_code(cand + NEG_INF_CODE).astype(BF16), ge)
        return jnp.where(cnt >= top, cand, tu)

    lo_code = lax.fori_loop(0, 16, coarse_body, jnp.zeros((1, qb), I32)) + NEG_INF_CODE
    lo_f = f32_of_code(lo_code)
    hi_f = f32_of_code(lo_code + 1)
    lo_b = lo_f.astype(BF16)
    hi_b = hi_f.astype(BF16)
    keep, drop = jnp.zeros((), BF16), jnp.full((), -jnp.inf, BF16)
    n_above = _count16(sb_ref, skv, hi_b, ge)
    width = hi_f - lo_f
    usable = jnp.logical_and(width > 1e-30, width < 1e30)
    scale = jnp.where(usable, 65536.0 / jnp.where(usable, width, 1.0), 0.0)
    outside = jnp.int16(-HALF16)

    def residuals(c, carry):
        r0 = pl.multiple_of(c * ck, ck)
        s = sc_ref[pl.ds(r0, ck), :]
        x = jnp.clip(jnp.where(s > lo_f, s - lo_f, 0.0) * scale, 0.0, 65534.0)
        rq = (x.astype(I32) - (HALF16 - 1)).astype(jnp.int16)
        sb = sb_ref[pl.ds(r0, ck), :]
        inside = jnp.logical_and(sb >= lo_b, sb < hi_b)
        lom_ref[pl.ds(r0, ck), :] = jnp.where(inside, rq, outside)
        return carry

    lax.fori_loop(0, nc, residuals, 0)

    def fine_body(i, tu):
        cand = tu | jnp.left_shift(jnp.int32(1), 15 - i)
        cnt = _count16(lom_ref, skv, (cand - HALF16).astype(jnp.int16), ge)
        return jnp.where(cnt >= top - n_above, cand, tu)

    cut = (lax.fori_loop(0, 16, fine_body, jnp.zeros((1, qb), I32)) - HALF16).astype(jnp.int16)

    def bias_default(c, acc):
        r0 = pl.multiple_of(c * ck, ck)
        at_cut = lom_ref[pl.ds(r0, ck), :] >= cut
        sel = jnp.logical_or(sb_ref[pl.ds(r0, ck), :] >= hi_b, at_cut)
        bias_ref[pl.ds(r0, ck), :] = jnp.where(sel, keep, drop)
        ones = jnp.where(at_cut, jnp.int16(1), jnp.int16(0))
        parts = [ones[i * PACK16:(i + 1) * PACK16, :] for i in range(ck // PACK16)]
        while len(parts) > 1:
            parts = [a + b for a, b in zip(parts[0::2], parts[1::2])]
        return acc + parts[0]

    n_cut = lax.fori_loop(0, nc, bias_default, jnp.zeros((PACK16, qb), jnp.int16))
    n_ge = n_above + jnp.sum(n_cut.astype(I32), axis=0, keepdims=True)
    tie = jnp.logical_and(n_ge > top, lo_code != NEG_INF_CODE)
    any_tie = jnp.max(jnp.where(tie, 1, 0).astype(I32)) > 0

    @pl.when(any_tie)
    def _tie():
        nbits = (skv - 1).bit_length()
        n_over = _count16(lom_ref, skv, cut, lambda a, b: a > b)
        need = top - n_above - n_over
        big = jnp.int16(2 ** nbits)

        def fill(c, carry):
            r0 = pl.multiple_of(c * ck, ck)
            idx = (r0 + s_iota).astype(jnp.int16)
            tie_ref[pl.ds(r0, ck), :] = jnp.where(lom_ref[pl.ds(r0, ck), :] == cut, idx, big)
            return carry

        lax.fori_loop(0, nc, fill, 0)

        def bit2(i, m):
            cand = m | jnp.left_shift(jnp.int32(1), nbits - 1 - i)
            cnt = _count16(tie_ref, skv, cand.astype(jnp.int16), lt)
            return jnp.where(cnt < need, cand, m)

        m = lax.fori_loop(0, nbits, bit2, jnp.zeros((1, qb), I32)).astype(jnp.int16)

        def bias_tie(c, carry):
            r0 = pl.multiple_of(c * ck, ck)
            sel = jnp.logical_or(sb_ref[pl.ds(r0, ck), :] >= hi_b,
                                 lom_ref[pl.ds(r0, ck), :] > cut)
            sel = jnp.logical_or(sel, tie_ref[pl.ds(r0, ck), :] <= m)
            bias_ref[pl.ds(r0, ck), :] = jnp.where(sel, keep, drop)
            return carry

        lax.fori_loop(0, nc, bias_tie, 0)

    _softmax_pv(skv, j, q_ref, kz_ref, o_ref, vt_ref, bias_ref, l_ref, p_ref)


def _softmax_pv(skv, j, q_ref, kz_ref, o_ref, vt_ref, bias_ref, l_ref, p_ref):
    ck = ATT_CHUNK
    nc = skv // ck
    mm_chunks = [(r0, min(ATT_MM_CHUNK, skv - r0)) for r0 in range(0, skv, ATT_MM_CHUNK)]
    sub = ATT_SUB
    qb = LANES
    t_sub = j * qb + lax.broadcasted_iota(I32, (sub, qb), 1)
    s_sub = lax.broadcasted_iota(I32, (sub, qb), 0)
    qt = _transposed_chunks(q_ref[...])
    o_parts = []
    for g in range(N_KV_HEADS):
        qet = qt[:, (2 * g) * qb:(2 * g + 2) * qb]
        m8 = jnp.full((SUBLANES, 4 * qb), -jnp.inf, F32)
        for r0, mk in mm_chunks:
            le = _dot(kz_ref[r0:r0 + mk, (2 * g) * LANES:(2 * g + 1) * LANES], qet)
            lo = _dot(kz_ref[r0:r0 + mk, (2 * g + 1) * LANES:(2 * g + 2) * LANES], qet)
            for s0 in range(0, mk, sub):
                rows = slice(r0 + s0, r0 + s0 + sub)
                b = bias_ref[rows, :].astype(F32)
                if r0 + s0 + sub > skv - KV_CLASS:
                    b = jnp.where((r0 + s0) + s_sub <= t_sub, b, -jnp.inf)
                l = jnp.concatenate([le[s0:s0 + sub] + jnp.concatenate([b, b], axis=1),
                                     lo[s0:s0 + sub] + jnp.concatenate([b, b], axis=1)], axis=1)
                l_ref[rows, :] = l
                m8 = jnp.maximum(m8, jnp.max(l.reshape(sub // SUBLANES, SUBLANES, 4 * qb), axis=0))
        mx = jnp.max(m8, axis=0, keepdims=True)

        def pbody(c, carry, mx=mx):
            r0 = pl.multiple_of(c * ck, ck)
            p_ref[pl.ds(r0, ck), :] = jnp.exp2(l_ref[pl.ds(r0, ck), :] - mx).astype(BF16)
            return carry

        lax.fori_loop(0, nc, pbody, 0)
        ot = _dot(vt_ref[g, :, 0:skv], p_ref[0:skv, :])
        o_parts.append(ot[0:HEAD_DIM, :] * (1.0 / ot[HEAD_DIM:HEAD_DIM + 1, :]))
    ot = jnp.concatenate(o_parts, axis=0)
    y = jnp.concatenate([ot[:, i * qb:(i + 1) * qb].T for i in range(4)], axis=1)
    o_ref[...] = y.astype(BF16)


def _attention(q, kz, v, qi, kiz, wi, seq, top):
    qb = LANES
    step_rows = min(seq, KV_CLASS)
    sq = pl.Squeezed()
    per_q = lambda c: pl.BlockSpec((sq, step_rows, c), lambda b, j: (b, j, 0))
    per_b = lambda c: pl.BlockSpec((sq, seq, c), lambda b, j: (b, 0, 0))
    return pl.pallas_call(
        functools.partial(_attn_kernel, seq, top),
        out_shape=jax.ShapeDtypeStruct((BATCH, seq, D_ATTN), BF16),
        grid=(BATCH, seq // step_rows),
        in_specs=[per_q(D_ATTN), per_b(4 * LANES), per_b(KV_DIM), per_q(512), per_b(2 * LANES), per_q(LANES)],
        out_specs=per_q(D_ATTN),
        scratch_shapes=[
            pltpu.VMEM((N_KV_HEADS, HEAD_DIM + PACK16, seq), BF16),
            pltpu.VMEM((seq, qb), F32),
            pltpu.VMEM((seq, qb), BF16),
            pltpu.VMEM((seq, qb), jnp.int16),
            pltpu.VMEM((seq, qb), jnp.int16),
            pltpu.VMEM((seq, qb), BF16),
            pltpu.VMEM((seq, 4 * qb), F32),
            pltpu.VMEM((seq, 4 * qb), BF16),
        ],
        compiler_params=pltpu.CompilerParams(dimension_semantics=("parallel", "arbitrary"),
                                             vmem_limit_bytes=VMEM_LIMIT),
        name="dsa_attention",
    )(q, kz, v, qi, kiz, wi)


def _mix_ffn_kernel(ys_ref, ya_ref, h_ref, p_ref, wo_ref, g1_ref, b1_ref,
                    wup_ref, wd_ref, wpg_ref, wpp_ref, g2_ref, b2_ref, o_ref):
    mix = _dot(ys_ref[...], wo_ref[0:2 * D_S5, :]) + _dot(ya_ref[...], wo_ref[2 * D_S5:, :])
    h = _ln(ALPHA * h_ref[...] + mix, g1_ref[...], b1_ref[...])
    hb = h.astype(BF16)
    gate = _dot(hb, wup_ref[:, 0:D_FF])
    up = _dot(hb, wup_ref[:, D_FF:])
    act = (gate * _sigmoid(gate) * up).astype(BF16)
    ffn = _dot(act, wd_ref[...])
    ple = _sigmoid(_dot(hb, wpg_ref[...])) * _dot(p_ref[...].astype(BF16), wpp_ref[...])
    o_ref[...] = _ln(ALPHA * h + ffn + ple, g2_ref[...], b2_ref[...])


def _mix_ffn(layer, y_scan, y_at, h, p, w_out_p, g1, b1, w_up, w_down, w_pg, w_pp, g2, b2):
    t = h.shape[0]
    tm = min(t, 512)
    row = lambda c: pl.BlockSpec((tm, c), lambda i: (i, 0))
    const = lambda shape: pl.BlockSpec(shape, lambda i: (0, 0), pipeline_mode=pl.Buffered(1))
    of_layer = lambda r, c: pl.BlockSpec((pl.Squeezed(), r, c), lambda i: (layer, 0, 0),
                                         pipeline_mode=pl.Buffered(1))
    vec = _full_spec((1, D_MODEL))
    return pl.pallas_call(
        _mix_ffn_kernel,
        out_shape=jax.ShapeDtypeStruct((t, D_MODEL), F32),
        grid=(t // tm,),
        in_specs=[row(2 * D_S5), row(D_ATTN), row(D_MODEL), pl.BlockSpec((pl.Squeezed(), tm, D_PLE), lambda i: (layer, i, 0)),
                  const((D_MIX, D_MODEL)), vec, vec,
                  of_layer(D_MODEL, 2 * D_FF), of_layer(D_FF, D_MODEL), of_layer(D_MODEL, D_MODEL),
                  of_layer(D_PLE, D_MODEL), vec, vec],
        out_specs=row(D_MODEL),
        compiler_params=pltpu.CompilerParams(dimension_semantics=("parallel",),
                                             vmem_limit_bytes=VMEM_LIMIT),
        name="mix_ffn_ln",
    )(y_scan, y_at, h, p, w_out_p, g1, b1, w_up, w_down, w_pg, w_pp, g2, b2)


def _block_diag(w):
    hh, n, m = w.shape
    eye = jnp.eye(hh, dtype=w.dtype)
    return jnp.einsum('hij,hk->hikj', w, eye).reshape(hh * n, hh * m)


def _s5_out_matrix(c_re, c_im):
    cre = _block_diag(c_re.transpose(0, 2, 1))
    cim = _block_diag(c_im.transpose(0, 2, 1))
    return jnp.concatenate([cre, -cim], axis=0).astype(BF16)


def _attn_out_rows(w_at):
    w = w_at.reshape(N_KV_HEADS, 4, HEAD_DIM, D_MODEL)
    w = w[:, jnp.array([0, 2, 1, 3])]
    return w.transpose(1, 0, 2, 3).reshape(D_ATTN, D_MODEL)


def _forward(seq, top, x, p, positions, ln_emb_g, ln_emb_b, w_in,
             s5_lam_re, s5_lam_im, s5_log_step, s5_b_re, s5_b_im, s5_c_re, s5_c_im,
             s5_d, s5_w_glu, s5_b_glu,
             rg_conv_w, rg_conv_b, rg_wa, rg_ba, rg_wx, rg_bx, rg_lam,
             w_out, ln1_g, ln1_b, ffn_w_up, ffn_w_down, ple_w_gate, ple_w_proj,
             ln2_g, ln2_b):
    t = BATCH * seq
    cos, sin = _rope_tables(positions, seq)
    row = lambda a: a.reshape(1, -1)
    p_rows = p.reshape(DEPTH, t, D_PLE)
    w_up_b, w_down_b = ffn_w_up.astype(BF16), ffn_w_down.astype(BF16)
    w_pg_b, w_pp_b = ple_w_gate.astype(BF16), ple_w_proj.astype(BF16)
    h3 = x
    for i in range(DEPTH):
        w_in_p = jnp.pad(w_in[i], ((0, 0), (0, N_IN_PAD - N_IN))).astype(BF16)
        outs = _inproj(h3, w_in_p, cos, sin, row(ln_emb_g), row(ln_emb_b), i == 0, seq)
        if i == 0:
            h3, outs = outs[0], outs[1:]
        scan_in, q, kz, v, qi, kiz, wi = outs
        a8, bd = _s5_params(s5_lam_re[i], s5_lam_im[i], s5_log_step[i], s5_b_re[i], s5_b_im[i])
        y_scan = _scans(scan_in, a8, bd, _s5_out_matrix(s5_c_re[i], s5_c_im[i]), row(s5_d[i]),
                        s5_w_glu[i].astype(BF16), row(s5_b_glu[i]),
                        rg_conv_w[i], row(rg_conv_b[i]),
                        _block_diag(rg_wa[i]).astype(BF16), row(rg_ba[i]),
                        _block_diag(rg_wx[i]).astype(BF16), row(rg_bx[i]), row(rg_lam[i]), seq)
        y_at = _attention(q, kz, v, qi, kiz, wi, seq, top)
        w_out_p = jnp.concatenate([w_out[i][:2 * D_S5], _attn_out_rows(w_out[i][2 * D_S5:])], axis=0).astype(BF16)
        h2 = _mix_ffn(i, y_scan.reshape(t, 2 * D_S5), y_at.reshape(t, D_ATTN), h3.reshape(t, D_MODEL),
                      p_rows, w_out_p, row(ln1_g[i]), row(ln1_b[i]),
                      w_up_b, w_down_b, w_pg_b, w_pp_b, row(ln2_g[i]), row(ln2_b[i]))
        h3 = h2.reshape(BATCH, seq, D_MODEL)
    return h3


def kernel(x, p, positions, ln_emb_g, ln_emb_b, w_in, s5_lam_re, s5_lam_im, s5_log_step, s5_b_re, s5_b_im, s5_c_re, s5_c_im, s5_d, s5_w_glu, s5_b_glu, rg_conv_w, rg_conv_b, rg_wa, rg_ba, rg_wx, rg_bx, rg_lam, w_out, ln1_g, ln1_b, ffn_w_up, ffn_w_down, ple_w_gate, ple_w_proj, ln2_g, ln2_b):
    seq = x.shape[1]
    return _forward(seq, min(TOPK_MAX, seq // 4), x, p, positions, ln_emb_g, ln_emb_b, w_in,
                    s5_lam_re, s5_lam_im, s5_log_step, s5_b_re, s5_b_im, s5_c_re, s5_c_im,
                    s5_d, s5_w_glu, s5_b_glu,
                    rg_conv_w, rg_conv_b, rg_wa, rg_ba, rg_wx, rg_bx, rg_lam,
                    w_out, ln1_g, ln1_b, ffn_w_up, ffn_w_down, ple_w_gate, ple_w_proj,
                    ln2_g, ln2_b)
```

```python
import functools
import math

import jax
import jax.numpy as jnp
from jax import lax
from jax.experimental import pallas as pl
from jax.experimental.pallas import tpu as pltpu

F32 = jnp.float32
BF16 = jnp.bfloat16
I32 = jnp.int32

D_MODEL = 1024
BATCH = 8
SEQ = 2048
DEPTH = 2
D_S5 = 256
S5_GROUP = 16
S5_GROUPS = 16
S5_STATE = 64
N_S5_STATE = S5_GROUPS * S5_STATE
D_RG = 256
RG_BLOCKS = 8
RG_BLOCK = 32
RG_CONV = 4
RG_C = 8.0
N_HEADS = 8
N_KV_HEADS = 2
HEAD_DIM = 64
D_ATTN = 512
KV_DIM = 128
IDX_HEADS = 8
IDX_DIM = 64
TOPK_MAX = 256
D_MIX = 1024
ROPE_THETA = 10000.0
D_FF = 2816
D_PLE = 256
ALPHA = (2.0 * DEPTH) ** 0.25
LN_EPS = 1e-5
N_IN = 2120
N_IN_PAD = 2176
ATT_SCALE = HEAD_DIM ** -0.5
IDX_SCALE = (IDX_HEADS * IDX_DIM) ** -0.5
LOG2E = math.log2(math.e)

LANES = 128
SUBLANES = 8
PACK16 = 16
INPROJ_BATCHES = 4
HALF16 = 32768
KV_CLASS = 256
ATT_CHUNK = 256
ATT_MM_CHUNK = 512
ATT_SUB = 64
NEG_INF_CODE = 127

VMEM_LIMIT = 56 * 1024 * 1024


def _dot(a, b):
    return jnp.dot(a, b, preferred_element_type=F32)


def _dot_nt(a, b):
    return lax.dot_general(a, b, (((1,), (1,)), ((), ())), preferred_element_type=F32)


def _ln(x, g, b):
    mu = jnp.mean(x, axis=-1, keepdims=True)
    xc = x - mu
    var = jnp.mean(xc * xc, axis=-1, keepdims=True)
    return xc * lax.rsqrt(var + LN_EPS) * g + b


def _gelu(x):
    c = math.sqrt(2.0 / math.pi)
    return 0.5 * x * (1.0 + jnp.tanh(c * (x + 0.044715 * (x * x * x))))


def _sigmoid(x):
    return 1.0 / (1.0 + jnp.exp(-x))


def _full_spec(shape):
    n = len(shape)
    return pl.BlockSpec(shape, lambda *_: (0,) * n)


def _rope_table_kernel(pos_ref, inv_ref, sgn_ref, cos_ref, sin_ref):
    quart = LANES // 4
    q = pos_ref.shape[0] // 4
    lane = lax.broadcasted_iota(I32, (q, LANES), 1)
    pos4 = None
    for k in range(4):
        pk = jnp.broadcast_to(pos_ref[k * q:(k + 1) * q, :].astype(F32), (q, LANES))
        pos4 = pk if pos4 is None else jnp.where(lane >= k * quart, pk, pos4)
    ang = pos4 * inv_ref[...]
    c4 = jnp.cos(ang)
    s4 = jnp.sin(ang)
    first = lane < quart

    def spread(x, k):
        y = x if k == 0 else pltpu.roll(x, LANES - quart * k, 1)
        y = jnp.where(first, y, 0.0)
        y = y + pltpu.roll(y, quart, 1)
        return y + pltpu.roll(y, 2 * quart, 1)

    for k in range(4):
        cos_ref[k * q:(k + 1) * q, :] = spread(c4, k)
        sin_ref[k * q:(k + 1) * q, :] = spread(s4, k) * sgn_ref[...]


def _rope_tables(positions, seq):
    t = BATCH * seq
    tm = min(t, 2048)
    inv = ROPE_THETA ** (-jnp.arange(0, HEAD_DIM, 2, dtype=F32) / HEAD_DIM)
    inv128 = jnp.tile(inv, 4)[None, :]
    sgn = jnp.where((jnp.arange(LANES) % HEAD_DIM) < HEAD_DIM // 2, -1.0, 1.0).astype(F32)[None, :]
    pos = positions.reshape(t, 1)
    cos, sin = pl.pallas_call(
        _rope_table_kernel,
        out_shape=(jax.ShapeDtypeStruct((t, LANES), F32),) * 2,
        grid=(t // tm,),
        in_specs=[pl.BlockSpec((tm, 1), lambda i: (i, 0)), _full_spec((1, LANES)), _full_spec((1, LANES))],
        out_specs=(pl.BlockSpec((tm, LANES), lambda i: (i, 0)),) * 2,
        compiler_params=pltpu.CompilerParams(dimension_semantics=("parallel",)),
        name="rope_tables",
    )(pos, inv128, sgn)
    return cos.reshape(BATCH, seq, LANES), sin.reshape(BATCH, seq, LANES)


def _inproj_kernel(apply_ln, tt, h_ref, w_ref, cos_ref, sin_ref, g_ref, b_ref, *outs):
    nb = INPROJ_BATCHES
    for b0 in range(0, BATCH, nb):
        _inproj_rows(apply_ln, tt, b0, nb, h_ref, w_ref, cos_ref, sin_ref, g_ref, b_ref, outs)


def _inproj_rows(apply_ln, tt, b0, nb, h_ref, w_ref, cos_ref, sin_ref, g_ref, b_ref, outs):
    if apply_ln:
        hn_ref, so_ref, q_ref, kz_ref, v_ref, qi_ref, kiz_ref, wi_ref = outs
    else:
        so_ref, q_ref, kz_ref, v_ref, qi_ref, kiz_ref, wi_ref = outs
    bs = slice(b0, b0 + nb)
    rows = nb * tt
    h = h_ref[bs].reshape(rows, D_MODEL)
    if apply_ln:
        h = _ln(h, g_ref[...], b_ref[...])
        hn_ref[bs] = h.reshape(nb, tt, D_MODEL)
    hb = h.astype(BF16)
    cos = cos_ref[bs].reshape(rows, LANES)
    sin = sin_ref[bs].reshape(rows, LANES)
    lane = lax.broadcasted_iota(I32, (rows, LANES), 1)
    first_half = (lane & (HEAD_DIM // 2)) == 0
    lo64 = lane < HEAD_DIM

    def rope(x, cs, sn):
        partner = jnp.where(first_half, pltpu.roll(x, LANES - 32, 1), pltpu.roll(x, 32, 1))
        return x * cs + partner * sn

    def rope_wide(x, n_chunks):
        return jnp.concatenate(
            [rope(x[:, c * LANES:(c + 1) * LANES], cos, sin) for c in range(n_chunks)], axis=1)

    ps = _dot(hb, w_ref[:, 0:768])
    for j in range(6):
        for b in range(nb):
            so_ref[j, pl.ds(b0 + b, tt, stride=BATCH), :] = ps[b * tt:(b + 1) * tt, j * LANES:(j + 1) * LANES]

    pq = _dot(hb, w_ref[:, 768:1280])
    q_ref[bs] = (rope_wide(pq, 4) * (ATT_SCALE * LOG2E)).astype(BF16).reshape(nb, tt, D_ATTN)

    pkv = _dot(hb, w_ref[:, 1280:1536])
    kr = rope(pkv[:, 0:LANES], cos, sin)
    ksw = pltpu.roll(kr, HEAD_DIM, 1)
    zero = jnp.zeros_like(kr)
    kz = jnp.concatenate([jnp.where(lo64, kr, zero), jnp.where(lo64, zero, ksw),
                          jnp.where(lo64, ksw, zero), jnp.where(lo64, zero, kr)], axis=1)
    kz_ref[bs] = kz.astype(BF16).reshape(nb, tt, 4 * LANES)
    v_ref[bs] = pkv[:, LANES:2 * LANES].astype(BF16).reshape(nb, tt, KV_DIM)

    pqi = _dot(hb, w_ref[:, 1536:2048])
    qi_ref[bs] = rope_wide(pqi, 4).astype(BF16).reshape(nb, tt, IDX_HEADS * IDX_DIM)

    pk = _dot(hb, w_ref[:, 2048:N_IN_PAD])
    kir = rope(pk, jnp.where(lo64, cos, 1.0), jnp.where(lo64, sin, 0.0))
    kie = jnp.where(lo64, kir, zero)
    kiz = jnp.concatenate([kie, pltpu.roll(kie, HEAD_DIM, 1)], axis=1)
    kiz_ref[bs] = kiz.astype(BF16).reshape(nb, tt, 2 * LANES)
    wi_ref[bs] = pk.reshape(nb, tt, LANES)


def _inproj(h3, w_in_p, cos, sin, ln_g, ln_b, apply_ln, seq):
    tt = min(seq, 128)
    nt = seq // tt
    blk = lambda c: pl.BlockSpec((BATCH, tt, c), lambda i: (0, i, 0))
    out_shape = [
        jax.ShapeDtypeStruct((6, seq * BATCH, LANES), F32),
        jax.ShapeDtypeStruct((BATCH, seq, D_ATTN), BF16),
        jax.ShapeDtypeStruct((BATCH, seq, 4 * LANES), BF16),
        jax.ShapeDtypeStruct((BATCH, seq, KV_DIM), BF16),
        jax.ShapeDtypeStruct((BATCH, seq, IDX_HEADS * IDX_DIM), BF16),
        jax.ShapeDtypeStruct((BATCH, seq, 2 * LANES), BF16),
        jax.ShapeDtypeStruct((BATCH, seq, LANES), F32),
    ]
    out_specs = [pl.BlockSpec((6, tt * BATCH, LANES), lambda i: (0, i, 0)),
                 blk(D_ATTN), blk(4 * LANES), blk(KV_DIM), blk(512), blk(2 * LANES), blk(LANES)]
    if apply_ln:
        out_shape = [jax.ShapeDtypeStruct((BATCH, seq, D_MODEL), F32)] + out_shape
        out_specs = [blk(D_MODEL)] + out_specs
    return pl.pallas_call(
        functools.partial(_inproj_kernel, apply_ln, tt),
        out_shape=tuple(out_shape),
        grid=(nt,),
        in_specs=[blk(D_MODEL), _full_spec((D_MODEL, N_IN_PAD)), blk(LANES), blk(LANES),
                  _full_spec((1, D_MODEL)), _full_spec((1, D_MODEL))],
        out_specs=tuple(out_specs),
        compiler_params=pltpu.CompilerParams(dimension_semantics=("parallel",),
                                             vmem_limit_bytes=VMEM_LIMIT),
        name="inproj_ln" if apply_ln else "inproj",
    )(h3, w_in_p, cos, sin, ln_g, ln_b)


def _s5_param_kernel(lr_ref, li_ref, ls_ref, br_ref, bi_ref, a_ref, bd_ref):
    lr = lr_ref[...]
    li = li_ref[...]
    step = jnp.exp(ls_ref[...])
    mag = jnp.exp(lr * step)
    ar = mag * jnp.cos(li * step)
    ai = mag * jnp.sin(li * step)
    den = lr * lr + li * li
    nr, ni = ar - 1.0, ai
    cr = (nr * lr + ni * li) / den
    ci = (ni * lr - nr * li) / den
    br = br_ref[...]
    bi = bi_ref[...]
    bbr = cr * br - ci * bi
    bbi = cr * bi + ci * br
    row = lax.broadcasted_iota(I32, (D_S5, N_S5_STATE), 0)
    col = lax.broadcasted_iota(I32, (D_S5, N_S5_STATE), 1)
    blk = (row // S5_GROUP) == (col // S5_STATE)
    zero = jnp.zeros((D_S5, N_S5_STATE), F32)
    bd_ref[:, 0:N_S5_STATE] = jnp.where(blk, jnp.concatenate([bbr] * S5_GROUPS, axis=0), zero).astype(BF16)
    bd_ref[:, N_S5_STATE:] = jnp.where(blk, jnp.concatenate([bbi] * S5_GROUPS, axis=0), zero).astype(BF16)
    a_ref[:, 0:N_S5_STATE] = jnp.broadcast_to(ar, (SUBLANES, N_S5_STATE))
    a_ref[:, N_S5_STATE:] = jnp.broadcast_to(ai, (SUBLANES, N_S5_STATE))


def _s5_params(lam_re, lam_im, log_step, b_re, b_im):
    lr = lam_re.reshape(1, N_S5_STATE)
    li = lam_im.reshape(1, N_S5_STATE)
    ls = jnp.repeat(log_step, S5_STATE).reshape(1, N_S5_STATE)
    br = b_re.transpose(2, 0, 1).reshape(S5_GROUP, N_S5_STATE)
    bi = b_im.transpose(2, 0, 1).reshape(S5_GROUP, N_S5_STATE)
    return pl.pallas_call(
        _s5_param_kernel,
        out_shape=(jax.ShapeDtypeStruct((SUBLANES, 2 * N_S5_STATE), F32),
                   jax.ShapeDtypeStruct((D_S5, 2 * N_S5_STATE), BF16)),
        name="s5_params",
    )(lr, li, ls, br, bi)


def _scan_kernel(tt, si_ref, a_ref, bd_ref, cd_ref, dsk_ref, wglu_ref, bglu_ref,
                 cw_ref, cb_ref, wa_ref, ba_ref, wx_ref, bx_ref, lam_ref,
                 o_ref, x_ref, hs_ref, halo_ref, ab_ref, hrg_ref, y_ref):
    rows = BATCH * tt
    i = pl.program_id(0)

    @pl.when(i == 0)
    def _init():
        hs_ref[...] = jnp.zeros_like(hs_ref)
        halo_ref[...] = jnp.zeros_like(halo_ref)
        hrg_ref[...] = jnp.zeros_like(hrg_ref)

    u = jnp.concatenate([si_ref[0], si_ref[1]], axis=1)
    x_ref[...] = _dot(u.astype(BF16), bd_ref[...])
    half = N_S5_STATE // 2
    for hh in range(2):
        lo = hh * half
        ar = a_ref[:, lo:lo + half]
        ai = a_ref[:, N_S5_STATE + lo:N_S5_STATE + lo + half]

        def step(t, carry, lo=lo, ar=ar, ai=ai):
            hr, hi = carry
            r0 = pl.multiple_of(t * BATCH, BATCH)
            br = x_ref[pl.ds(r0, BATCH), lo:lo + half]
            bi = x_ref[pl.ds(r0, BATCH), N_S5_STATE + lo:N_S5_STATE + lo + half]
            nr = ar * hr - ai * hi + br
            ni = ar * hi + ai * hr + bi
            x_ref[pl.ds(r0, BATCH), lo:lo + half] = nr
            x_ref[pl.ds(r0, BATCH), N_S5_STATE + lo:N_S5_STATE + lo + half] = ni
            return nr, ni

        hr, hi = lax.fori_loop(
            0, tt, step,
            (hs_ref[:, lo:lo + half], hs_ref[:, N_S5_STATE + lo:N_S5_STATE + lo + half]),
            unroll=True)
        hs_ref[:, lo:lo + half] = hr
        hs_ref[:, N_S5_STATE + lo:N_S5_STATE + lo + half] = hi

    y = _dot(x_ref[...].astype(BF16), cd_ref[...]) + dsk_ref[...] * u
    y = _gelu(y)
    y = y * _sigmoid(_dot(y.astype(BF16), wglu_ref[...]) + bglu_ref[...])
    y_ref[0] = y[:, 0:LANES]
    y_ref[1] = y[:, LANES:2 * LANES]

    xr = jnp.concatenate([si_ref[2], si_ref[3]], axis=1)
    gate = jnp.concatenate([si_ref[4], si_ref[5]], axis=1)
    hal = (RG_CONV - 1) * BATCH
    xext = jnp.concatenate([halo_ref[...], xr], axis=0)
    halo_ref[...] = xr[rows - hal:rows, :]
    xc = cb_ref[...]
    for k in range(RG_CONV):
        xc = xc + cw_ref[k:k + 1, :] * xext[k * BATCH:k * BATCH + rows, :]
    xcb = xc.astype(BF16)
    r = _sigmoid(_dot(xcb, wa_ref[...]) + ba_ref[...])
    ig = _sigmoid(_dot(xcb, wx_ref[...]) + bx_ref[...])
    nl = -lam_ref[...]
    softplus = jnp.maximum(nl, 0.0) + jnp.log(1.0 + jnp.exp(-jnp.abs(nl)))
    log_a = -RG_C * r * softplus
    a = jnp.exp(log_a)
    mult = jnp.sqrt(1.0 - a * a)
    ab_ref[0] = a
    ab_ref[1] = mult * (ig * xc)

    def rg_step(t, h):
        r0 = pl.multiple_of(t * BATCH, BATCH)
        hn = ab_ref[0, pl.ds(r0, BATCH), :] * h + ab_ref[1, pl.ds(r0, BATCH), :]
        ab_ref[1, pl.ds(r0, BATCH), :] = hn
        return hn

    hrg_ref[...] = lax.fori_loop(0, tt, rg_step, hrg_ref[...], unroll=True)
    yr = ab_ref[1] * _gelu(gate)
    y_ref[2] = yr[:, 0:LANES]
    y_ref[3] = yr[:, LANES:2 * LANES]

    for b in range(BATCH):
        o_ref[b] = jnp.concatenate(
            [y_ref[j, pl.ds(b, tt, stride=BATCH), :] for j in range(4)], axis=1).astype(BF16)


def _scans(scan_in, a8, bd, cd, dsk, wglu, bglu, cw, cb, wa, ba, wx, bx, lam, seq):
    tt = min(seq, 128)
    nt = seq // tt
    rows = tt * BATCH
    params = [a8, bd, cd, dsk, wglu, bglu, cw, cb, wa, ba, wx, bx, lam]
    return pl.pallas_call(
        functools.partial(_scan_kernel, tt),
        out_shape=jax.ShapeDtypeStruct((BATCH, seq, 2 * D_S5), BF16),
        grid=(nt,),
        in_specs=[pl.BlockSpec((6, rows, LANES), lambda i: (0, i, 0))] + [_full_spec(p.shape) for p in params],
        out_specs=pl.BlockSpec((BATCH, tt, 2 * D_S5), lambda i: (0, i, 0)),
        scratch_shapes=[
            pltpu.VMEM((rows, 2 * N_S5_STATE), F32),
            pltpu.VMEM((BATCH, 2 * N_S5_STATE), F32),
            pltpu.VMEM(((RG_CONV - 1) * BATCH, D_RG), F32),
            pltpu.VMEM((2, rows, D_RG), F32),
            pltpu.VMEM((BATCH, D_RG), F32),
            pltpu.VMEM((4, rows, LANES), F32),
        ],
        compiler_params=pltpu.CompilerParams(dimension_semantics=("arbitrary",),
                                             vmem_limit_bytes=VMEM_LIMIT),
        name="scans",
    )(scan_in, *params)


def _count16(ref, skv, cand, op):
    cb = jnp.broadcast_to(cand, (PACK16, LANES))
    accs = []
    for i in range(skv // PACK16):
        one = jnp.where(op(ref[i * PACK16:(i + 1) * PACK16, :], cb), jnp.int16(1), jnp.int16(0))
        if i < 4:
            accs.append(one)
        else:
            accs[i % 4] = accs[i % 4] + one
    tot = (accs[0] + accs[1]) + (accs[2] + accs[3])
    return jnp.sum(tot.astype(I32), axis=0, keepdims=True)


def _transposed_chunks(x):
    xf = x.astype(F32)
    return jnp.concatenate([xf[:, c * LANES:(c + 1) * LANES].T for c in range(4)], axis=1).astype(BF16)


def _attn_kernel(seq, top, q_ref, kz_ref, v_ref, qi_ref, kiz_ref, wi_ref, o_ref,
                 vt_ref, sc_ref, sb_ref, lom_ref, tie_ref, bias_ref, l_ref, p_ref):
    step = pl.program_id(1)

    @pl.when(step == 0)
    def _():
        vt = v_ref[...].astype(F32).T.astype(BF16)
        ones = jnp.where(lax.broadcasted_iota(I32, (PACK16, seq), 0) == 0, 1.0, 0.0).astype(BF16)
        for g in range(N_KV_HEADS):
            vt_ref[g, 0:HEAD_DIM, :] = vt[g * HEAD_DIM:(g + 1) * HEAD_DIM, :]
            vt_ref[g, HEAD_DIM:HEAD_DIM + PACK16, :] = ones

    cls_rows = min(seq, KV_CLASS)
    blocks = cls_rows // LANES
    for c in range(seq // cls_rows):
        @pl.when(step == c)
        def _(c=c):
            skv = cls_rows * (c + 1)

            def one_block(hb, carry):
                j = step * blocks + hb
                rows = pl.ds(pl.multiple_of(hb * LANES, LANES), LANES)
                qv, qiv, wiv, ov = q_ref.at[rows], qi_ref.at[rows], wi_ref.at[rows], o_ref.at[rows]
                if skv <= top:
                    bias_ref[0:skv, :] = jnp.zeros((skv, LANES), BF16)
                    _softmax_pv(skv, j, qv, kz_ref, ov, vt_ref, bias_ref, l_ref, p_ref)
                else:
                    _attn_class(skv, top, j, qv, kz_ref, qiv, kiz_ref, wiv, ov,
                                vt_ref, sc_ref, sb_ref, lom_ref, tie_ref, bias_ref, l_ref, p_ref)
                return carry

            lax.fori_loop(0, blocks, one_block, 0)


def _attn_class(skv, top, j, q_ref, kz_ref, qi_ref, kiz_ref, wi_ref, o_ref,
                vt_ref, sc_ref, sb_ref, lom_ref, tie_ref, bias_ref, l_ref, p_ref):
    ck = ATT_CHUNK
    nc = skv // ck
    mm_chunks = [(r0, min(ATT_MM_CHUNK, skv - r0)) for r0 in range(0, skv, ATT_MM_CHUNK)]
    sub = ATT_SUB
    qb = LANES
    t_idx = j * qb + lax.broadcasted_iota(I32, (ck, qb), 1)
    s_iota = lax.broadcasted_iota(I32, (ck, qb), 0)
    t_sub = j * qb + lax.broadcasted_iota(I32, (sub, qb), 1)
    s_sub = lax.broadcasted_iota(I32, (sub, qb), 0)

    wts = wi_ref[...].T[HEAD_DIM:HEAD_DIM + IDX_HEADS, :] * IDX_SCALE
    qit = _transposed_chunks(qi_ref[...])
    for r0, mk in mm_chunks:
        se = _dot(kiz_ref[r0:r0 + mk, 0:LANES], qit)
        so = _dot(kiz_ref[r0:r0 + mk, LANES:2 * LANES], qit)
        for s0 in range(0, mk, sub):
            acc = None
            for c4 in range(4):
                te = wts[2 * c4:2 * c4 + 1, :] * jnp.maximum(se[s0:s0 + sub, c4 * qb:(c4 + 1) * qb], 0.0)
                to = wts[2 * c4 + 1:2 * c4 + 2, :] * jnp.maximum(so[s0:s0 + sub, c4 * qb:(c4 + 1) * qb], 0.0)
                acc = te + to if acc is None else acc + (te + to)
            rows = slice(r0 + s0, r0 + s0 + sub)
            sc = acc
            if r0 + s0 + sub > skv - KV_CLASS:
                sc = jnp.where((r0 + s0) + s_sub <= t_sub, acc, -jnp.inf)
            sc_ref[rows, :] = sc
            near = sc.astype(BF16)
            rounded_up = (sc - near.astype(F32)).astype(BF16) < 0
            nb = pltpu.bitcast(near, jnp.int16)
            below = pltpu.bitcast(nb + jnp.where(nb < 0, jnp.int16(1), jnp.int16(-1)), BF16)
            sb_ref[rows, :] = jnp.where(rounded_up, below, near)

    ge = lambda a, b: a >= b
    lt = lambda a, b: a < b

    def f32_of_code(u):
        k = u - HALF16
        return pltpu.bitcast(jnp.left_shift(jnp.where(k < 0, k ^ 0x7FFF, k), 16), F32)

    def coarse_body(i, tu):
        cand = jnp.minimum(tu | jnp.left_shift(jnp.int32(1), 15 - i), 65535 - NEG_INF_CODE)
        cnt = _count16(sb_ref, skv, f32_of_code(cand + NEG_INF_CODE).astype(BF16), ge)
        return jnp.where(cnt >= top, cand, tu)

    lo_code = lax.fori_loop(0, 16, coarse_body, jnp.zeros((1, qb), I32)) + NEG_INF_CODE
    lo_f = f32_of_code(lo_code)
    hi_f = f32_of_code(lo_code + 1)
    lo_b = lo_f.astype(BF16)
    hi_b = hi_f.astype(BF16)
    keep, drop = jnp.zeros((), BF16), jnp.full((), -jnp.inf, BF16)
    n_above = _count16(sb_ref, skv, hi_b, ge)
    width = hi_f - lo_f
    usable = jnp.logical_and(width > 1e-30, width < 1e30)
    scale = jnp.where(usable, 65536.0 / jnp.where(usable, width, 1.0), 0.0)
    outside = jnp.int16(-HALF16)

    def residuals(c, carry):
        r0 = pl.multiple_of(c * ck, ck)
        s = sc_ref[pl.ds(r0, ck), :]
        x = jnp.clip(jnp.where(s > lo_f, s - lo_f, 0.0) * scale, 0.0, 65534.0)
        rq = (x.astype(I32) - (HALF16 - 1)).astype(jnp.int16)
        sb = sb_ref[pl.ds(r0, ck), :]
        inside = jnp.logical_and(sb >= lo_b, sb < hi_b)
        lom_ref[pl.ds(r0, ck), :] = jnp.where(inside, rq, outside)
        return carry

    lax.fori_loop(0, nc, residuals, 0)

    def fine_body(i, tu):
        cand = tu | jnp.left_shift(jnp.int32(1), 15 - i)
        cnt = _count16(lom_ref, skv, (cand - HALF16).astype(jnp.int16), ge)
        return jnp.where(cnt >= top - n_above, cand, tu)

    cut = (lax.fori_loop(0, 16, fine_body, jnp.zeros((1, qb), I32)) - HALF16).astype(jnp.int16)

    def bias_default(c, acc):
        r0 = pl.multiple_of(c * ck, ck)
        at_cut = lom_ref[pl.ds(r0, ck), :] >= cut
        sel = jnp.logical_or(sb_ref[pl.ds(r0, ck), :] >= hi_b, at_cut)
        bias_ref[pl.ds(r0, ck), :] = jnp.where(sel, keep, drop)
        ones = jnp.where(at_cut, jnp.int16(1), jnp.int16(0))
        parts = [ones[i * PACK16:(i + 1) * PACK16, :] for i in range(ck // PACK16)]
        while len(parts) > 1:
            parts = [a + b for a, b in zip(parts[0::2], parts[1::2])]
        return acc + parts[0]

    n_cut = lax.fori_loop(0, nc, bias_default, jnp.zeros((PACK16, qb), jnp.int16))
    n_ge = n_above + jnp.sum(n_cut.astype(I32), axis=0, keepdims=True)
    tie = jnp.logical_and(n_ge > top, lo_code != NEG_INF_CODE)
    any_tie = jnp.max(jnp.where(tie, 1, 0).astype(I32)) > 0

    @pl.when(any_tie)
    def _tie():
        nbits = (skv - 1).bit_length()
        n_over = _count16(lom_ref, skv, cut, lambda a, b: a > b)
        need = top - n_above - n_over
        big = jnp.int16(2 ** nbits)

        def fill(c, carry):
            r0 = pl.multiple_of(c * ck, ck)
            idx = (r0 + s_iota).astype(jnp.int16)
            tie_ref[pl.ds(r0, ck), :] = jnp.where(lom_ref[pl.ds(r0, ck), :] == cut, idx, big)
            return carry

        lax.fori_loop(0, nc, fill, 0)

        def bit2(i, m):
            cand = m | jnp.left_shift(jnp.int32(1), nbits - 1 - i)
            cnt = _count16(tie_ref, skv, cand.astype(jnp.int16), lt)
            return jnp.where(cnt < need, cand, m)

        m = lax.fori_loop(0, nbits, bit2, jnp.zeros((1, qb), I32)).astype(jnp.int16)

        def bias_tie(c, carry):
            r0 = pl.multiple_of(c * ck, ck)
            sel = jnp.logical_or(sb_ref[pl.ds(r0, ck), :] >= hi_b,
                                 lom_ref[pl.ds(r0, ck), :] > cut)
            sel = jnp.logical_or(sel, tie_ref[pl.ds(r0, ck), :] <= m)
            bias_ref[pl.ds(r0, ck), :] = jnp.where(sel, keep, drop)
            return carry

        lax.fori_loop(0, nc, bias_tie, 0)

    _softmax_pv(skv, j, q_ref, kz_ref, o_ref, vt_ref, bias_ref, l_ref, p_ref)


def _softmax_pv(skv, j, q_ref, kz_ref, o_ref, vt_ref, bias_ref, l_ref, p_ref):
    ck = ATT_CHUNK
    nc = skv // ck
    mm_chunks = [(r0, min(ATT_MM_CHUNK, skv - r0)) for r0 in range(0, skv, ATT_MM_CHUNK)]
    sub = ATT_SUB
    qb = LANES
    t_sub = j * qb + lax.broadcasted_iota(I32, (sub, qb), 1)
    s_sub = lax.broadcasted_iota(I32, (sub, qb), 0)
    qt = _transposed_chunks(q_ref[...])
    o_parts = []
    for g in range(N_KV_HEADS):
        qet = qt[:, (2 * g) * qb:(2 * g + 2) * qb]
        m8 = jnp.full((SUBLANES, 4 * qb), -jnp.inf, F32)
        for r0, mk in mm_chunks:
            le = _dot(kz_ref[r0:r0 + mk, (2 * g) * LANES:(2 * g + 1) * LANES], qet)
            lo = _dot(kz_ref[r0:r0 + mk, (2 * g + 1) * LANES:(2 * g + 2) * LANES], qet)
            for s0 in range(0, mk, sub):
                rows = slice(r0 + s0, r0 + s0 + sub)
                b = bias_ref[rows, :].astype(F32)
                if r0 + s0 + sub > skv - KV_CLASS:
                    b = jnp.where((r0 + s0) + s_sub <= t_sub, b, -jnp.inf)
                l = jnp.concatenate([le[s0:s0 + sub] + jnp.concatenate([b, b], axis=1),
                                     lo[s0:s0 + sub] + jnp.concatenate([b, b], axis=1)], axis=1)
                l_ref[rows, :] = l
                m8 = jnp.maximum(m8, jnp.max(l.reshape(sub // SUBLANES, SUBLANES, 4 * qb), axis=0))
        mx = jnp.max(m8, axis=0, keepdims=True)

        def pbody(c, carry, mx=mx):
            r0 = pl.multiple_of(c * ck, ck)
            p_ref[pl.ds(r0, ck), :] = jnp.exp2(l_ref[pl.ds(r0, ck), :] - mx).astype(BF16)
            return carry

        lax.fori_loop(0, nc, pbody, 0)
        ot = _dot(vt_ref[g, :, 0:skv], p_ref[0:skv, :])
        o_parts.append(ot[0:HEAD_DIM, :] * (1.0 / ot[HEAD_DIM:HEAD_DIM + 1, :]))
    ot = jnp.concatenate(o_parts, axis=0)
    y = jnp.concatenate([ot[:, i * qb:(i + 1) * qb].T for i in range(4)], axis=1)
    o_ref[...] = y.astype(BF16)


def _attention(q, kz, v, qi, kiz, wi, seq, top):
    qb = LANES
    step_rows = min(seq, KV_CLASS)
    sq = pl.Squeezed()
    per_q = lambda c: pl.BlockSpec((sq, step_rows, c), lambda b, j: (b, j, 0))
    per_b = lambda c: pl.BlockSpec((sq, seq, c), lambda b, j: (b, 0, 0))
    return pl.pallas_call(
        functools.partial(_attn_kernel, seq, top),
        out_shape=jax.ShapeDtypeStruct((BATCH, seq, D_ATTN), BF16),
        grid=(BATCH, seq // step_rows),
        in_specs=[per_q(D_ATTN), per_b(4 * LANES), per_b(KV_DIM), per_q(512), per_b(2 * LANES), per_q(LANES)],
        out_specs=per_q(D_ATTN),
        scratch_shapes=[
            pltpu.VMEM((N_KV_HEADS, HEAD_DIM + PACK16, seq), BF16),
            pltpu.VMEM((seq, qb), F32),
            pltpu.VMEM((seq, qb), BF16),
            pltpu.VMEM((seq, qb), jnp.int16),
            pltpu.VMEM((seq, qb), jnp.int16),
            pltpu.VMEM((seq, qb), BF16),
            pltpu.VMEM((seq, 4 * qb), F32),
            pltpu.VMEM((seq, 4 * qb), BF16),
        ],
        compiler_params=pltpu.CompilerParams(dimension_semantics=("parallel", "arbitrary"),
                                             vmem_limit_bytes=VMEM_LIMIT),
        name="dsa_attention",
    )(q, kz, v, qi, kiz, wi)


def _mix_ffn_kernel(ys_ref, ya_ref, h_ref, p_ref, wo_ref, g1_ref, b1_ref,
                    wup_ref, wd_ref, wpg_ref, wpp_ref, g2_ref, b2_ref, o_ref):
    mix = _dot(ys_ref[...], wo_ref[0:2 * D_S5, :]) + _dot(ya_ref[...], wo_ref[2 * D_S5:, :])
    h = _ln(ALPHA * h_ref[...] + mix, g1_ref[...], b1_ref[...])
    hb = h.astype(BF16)
    gate = _dot(hb, wup_ref[:, 0:D_FF])
    up = _dot(hb, wup_ref[:, D_FF:])
    act = (gate * _sigmoid(gate) * up).astype(BF16)
    ffn = _dot(act, wd_ref[...])
    ple = _sigmoid(_dot(hb, wpg_ref[...])) * _dot(p_ref[...].astype(BF16), wpp_ref[...])
    o_ref[...] = _ln(ALPHA * h + ffn + ple, g2_ref[...], b2_ref[...])


def _mix_ffn(layer, y_scan, y_at, h, p, w_out_p, g1, b1, w_up, w_down, w_pg, w_pp, g2, b2):
    t = h.shape[0]
    tm = min(t, 512)
    row = lambda c: pl.BlockSpec((tm, c), lambda i: (i, 0))
    const = lambda shape: pl.BlockSpec(shape, lambda i: (0, 0), pipeline_mode=pl.Buffered(1))
    of_layer = lambda r, c: pl.BlockSpec((pl.Squeezed(), r, c), lambda i: (layer, 0, 0),
                                         pipeline_mode=pl.Buffered(1))
    vec = _full_spec((1, D_MODEL))
    return pl.pallas_call(
        _mix_ffn_kernel,
        out_shape=jax.ShapeDtypeStruct((t, D_MODEL), F32),
        grid=(t // tm,),
        in_specs=[row(2 * D_S5), row(D_ATTN), row(D_MODEL), pl.BlockSpec((pl.Squeezed(), tm, D_PLE), lambda i: (layer, i, 0)),
                  const((D_MIX, D_MODEL)), vec, vec,
                  of_layer(D_MODEL, 2 * D_FF), of_layer(D_FF, D_MODEL), of_layer(D_MODEL, D_MODEL),
                  of_layer(D_PLE, D_MODEL), vec, vec],
        out_specs=row(D_MODEL),
        compiler_params=pltpu.CompilerParams(dimension_semantics=("parallel",),
                                             vmem_limit_bytes=VMEM_LIMIT),
        name="mix_ffn_ln",
    )(y_scan, y_at, h, p, w_out_p, g1, b1, w_up, w_down, w_pg, w_pp, g2, b2)


def _block_diag(w):
    hh, n, m = w.shape
    eye = jnp.eye(hh, dtype=w.dtype)
    return jnp.einsum('hij,hk->hikj', w, eye).reshape(hh * n, hh * m)


def _s5_out_matrix(c_re, c_im):
    cre = _block_diag(c_re.transpose(0, 2, 1))
    cim = _block_diag(c_im.transpose(0, 2, 1))
    return jnp.concatenate([cre, -cim], axis=0).astype(BF16)


def _attn_out_rows(w_at):
    w = w_at.reshape(N_KV_HEADS, 4, HEAD_DIM, D_MODEL)
    w = w[:, jnp.array([0, 2, 1, 3])]
    return w.transpose(1, 0, 2, 3).reshape(D_ATTN, D_MODEL)


def _forward(seq, top, x, p, positions, ln_emb_g, ln_emb_b, w_in,
             s5_lam_re, s5_lam_im, s5_log_step, s5_b_re, s5_b_im, s5_c_re, s5_c_im,
             s5_d, s5_w_glu, s5_b_glu,
             rg_conv_w, rg_conv_b, rg_wa, rg_ba, rg_wx, rg_bx, rg_lam,
             w_out, ln1_g, ln1_b, ffn_w_up, ffn_w_down, ple_w_gate, ple_w_proj,
             ln2_g, ln2_b):
    t = BATCH * seq
    cos, sin = _rope_tables(positions, seq)
    row = lambda a: a.reshape(1, -1)
    p_rows = p.reshape(DEPTH, t, D_PLE)
    w_up_b, w_down_b = ffn_w_up.astype(BF16), ffn_w_down.astype(BF16)
    w_pg_b, w_pp_b = ple_w_gate.astype(BF16), ple_w_proj.astype(BF16)
    h3 = x
    for i in range(DEPTH):
        w_in_p = jnp.pad(w_in[i], ((0, 0), (0, N_IN_PAD - N_IN))).astype(BF16)
        outs = _inproj(h3, w_in_p, cos, sin, row(ln_emb_g), row(ln_emb_b), i == 0, seq)
        if i == 0:
            h3, outs = outs[0], outs[1:]
        scan_in, q, kz, v, qi, kiz, wi = outs
        a8, bd = _s5_params(s5_lam_re[i], s5_lam_im[i], s5_log_step[i], s5_b_re[i], s5_b_im[i])
        y_scan = _scans(scan_in, a8, bd, _s5_out_matrix(s5_c_re[i], s5_c_im[i]), row(s5_d[i]),
                        s5_w_glu[i].astype(BF16), row(s5_b_glu[i]),
                        rg_conv_w[i], row(rg_conv_b[i]),
                        _block_diag(rg_wa[i]).astype(BF16), row(rg_ba[i]),
                        _block_diag(rg_wx[i]).astype(BF16), row(rg_bx[i]), row(rg_lam[i]), seq)
        y_at = _attention(q, kz, v, qi, kiz, wi, seq, top)
        w_out_p = jnp.concatenate([w_out[i][:2 * D_S5], _attn_out_rows(w_out[i][2 * D_S5:])], axis=0).astype(BF16)
        h2 = _mix_ffn(i, y_scan.reshape(t, 2 * D_S5), y_at.reshape(t, D_ATTN), h3.reshape(t, D_MODEL),
                      p_rows, w_out_p, row(ln1_g[i]), row(ln1_b[i]),
                      w_up_b, w_down_b, w_pg_b, w_pp_b, row(ln2_g[i]), row(ln2_b[i]))
        h3 = h2.reshape(BATCH, seq, D_MODEL)
    return h3


def kernel(x, p, positions, ln_emb_g, ln_emb_b, w_in, s5_lam_re, s5_lam_im, s5_log_step, s5_b_re, s5_b_im, s5_c_re, s5_c_im, s5_d, s5_w_glu, s5_b_glu, rg_conv_w, rg_conv_b, rg_wa, rg_ba, rg_wx, rg_bx, rg_lam, w_out, ln1_g, ln1_b, ffn_w_up, ffn_w_down, ple_w_gate, ple_w_proj, ln2_g, ln2_b):
    seq = x.shape[1]
    return _forward(seq, min(TOPK_MAX, seq // 4), x, p, positions, ln_emb_g, ln_emb_b, w_in,
                    s5_lam_re, s5_lam_im, s5_log_step, s5_b_re, s5_b_im, s5_c_re, s5_c_im,
                    s5_d, s5_w_glu, s5_b_glu,
                    rg_conv_w, rg_conv_b, rg_wa, rg_ba, rg_wx, rg_bx, rg_lam,
                    w_out, ln1_g, ln1_b, ffn_w_up, ffn_w_down, ple_w_gate, ple_w_proj,
                    ln2_g, ln2_b)
```

```python
import functools
import math

import jax
import jax.numpy as jnp
from jax import lax
from jax.experimental import pallas as pl
from jax.experimental.pallas import tpu as pltpu

F32 = jnp.float32
BF16 = jnp.bfloat16
I32 = jnp.int32

D_MODEL = 1024
BATCH = 8
SEQ = 2048
DEPTH = 2
D_S5 = 256
S5_GROUP = 16
S5_GROUPS = 16
S5_STATE = 64
N_S5_STATE = S5_GROUPS * S5_STATE
D_RG = 256
RG_BLOCKS = 8
RG_BLOCK = 32
RG_CONV = 4
RG_C = 8.0
N_HEADS = 8
N_KV_HEADS = 2
HEAD_DIM = 64
D_ATTN = 512
KV_DIM = 128
IDX_HEADS = 8
IDX_DIM = 64
TOPK_MAX = 256
D_MIX = 1024
ROPE_THETA = 10000.0
D_FF = 2816
D_PLE = 256
ALPHA = (2.0 * DEPTH) ** 0.25
LN_EPS = 1e-5
N_IN = 2120
N_IN_PAD = 2176
ATT_SCALE = HEAD_DIM ** -0.5
IDX_SCALE = (IDX_HEADS * IDX_DIM) ** -0.5
LOG2E = math.log2(math.e)

LANES = 128
SUBLANES = 8
PACK16 = 16
INPROJ_BATCHES = 4
HALF16 = 32768
KV_CLASS = 256
ATT_CHUNK = 256
ATT_MM_CHUNK = 512
ATT_SUB = 64
NEG_INF_CODE = 127

VMEM_LIMIT = 56 * 1024 * 1024


def _dot(a, b):
    return jnp.dot(a, b, preferred_element_type=F32)


def _dot_nt(a, b):
    return lax.dot_general(a, b, (((1,), (1,)), ((), ())), preferred_element_type=F32)


def _ln(x, g, b):
    mu = jnp.mean(x, axis=-1, keepdims=True)
    xc = x - mu
    var = jnp.mean(xc * xc, axis=-1, keepdims=True)
    return xc * lax.rsqrt(var + LN_EPS) * g + b


def _gelu(x):
    c = math.sqrt(2.0 / math.pi)
    return 0.5 * x * (1.0 + jnp.tanh(c * (x + 0.044715 * (x * x * x))))


def _sigmoid(x):
    return 0.5 * jnp.tanh(0.5 * x) + 0.5


def _full_spec(shape):
    n = len(shape)
    return pl.BlockSpec(shape, lambda *_: (0,) * n)


def _rope_table_kernel(pos_ref, inv_ref, sgn_ref, cos_ref, sin_ref):
    quart = LANES // 4
    q = pos_ref.shape[0] // 4
    lane = lax.broadcasted_iota(I32, (q, LANES), 1)
    pos4 = None
    for k in range(4):
        pk = jnp.broadcast_to(pos_ref[k * q:(k + 1) * q, :].astype(F32), (q, LANES))
        pos4 = pk if pos4 is None else jnp.where(lane >= k * quart, pk, pos4)
    ang = pos4 * inv_ref[...]
    c4 = jnp.cos(ang)
    s4 = jnp.sin(ang)
    first = lane < quart

    def spread(x, k):
        y = x if k == 0 else pltpu.roll(x, LANES - quart * k, 1)
        y = jnp.where(first, y, 0.0)
        y = y + pltpu.roll(y, quart, 1)
        return y + pltpu.roll(y, 2 * quart, 1)

    for k in range(4):
        cos_ref[k * q:(k + 1) * q, :] = spread(c4, k)
        sin_ref[k * q:(k + 1) * q, :] = spread(s4, k) * sgn_ref[...]


def _rope_tables(positions, seq):
    t = BATCH * seq
    tm = min(t, 2048)
    inv = ROPE_THETA ** (-jnp.arange(0, HEAD_DIM, 2, dtype=F32) / HEAD_DIM)
    inv128 = jnp.tile(inv, 4)[None, :]
    sgn = jnp.where((jnp.arange(LANES) % HEAD_DIM) < HEAD_DIM // 2, -1.0, 1.0).astype(F32)[None, :]
    pos = positions.reshape(t, 1)
    cos, sin = pl.pallas_call(
        _rope_table_kernel,
        out_shape=(jax.ShapeDtypeStruct((t, LANES), F32),) * 2,
        grid=(t // tm,),
        in_specs=[pl.BlockSpec((tm, 1), lambda i: (i, 0)), _full_spec((1, LANES)), _full_spec((1, LANES))],
        out_specs=(pl.BlockSpec((tm, LANES), lambda i: (i, 0)),) * 2,
        compiler_params=pltpu.CompilerParams(dimension_semantics=("parallel",)),
        name="rope_tables",
    )(pos, inv128, sgn)
    return cos.reshape(BATCH, seq, LANES), sin.reshape(BATCH, seq, LANES)


def _inproj_kernel(apply_ln, tt, h_ref, w_ref, cos_ref, sin_ref, g_ref, b_ref, *outs):
    nb = INPROJ_BATCHES
    for b0 in range(0, BATCH, nb):
        _inproj_rows(apply_ln, tt, b0, nb, h_ref, w_ref, cos_ref, sin_ref, g_ref, b_ref, outs)


def _inproj_rows(apply_ln, tt, b0, nb, h_ref, w_ref, cos_ref, sin_ref, g_ref, b_ref, outs):
    if apply_ln:
        hn_ref, so_ref, q_ref, kz_ref, v_ref, qi_ref, kiz_ref, wi_ref = outs
    else:
        so_ref, q_ref, kz_ref, v_ref, qi_ref, kiz_ref, wi_ref = outs
    bs = slice(b0, b0 + nb)
    rows = nb * tt
    h = h_ref[bs].reshape(rows, D_MODEL)
    if apply_ln:
        h = _ln(h, g_ref[...], b_ref[...])
        hn_ref[bs] = h.reshape(nb, tt, D_MODEL)
    hb = h.astype(BF16)
    cos = cos_ref[bs].reshape(rows, LANES)
    sin = sin_ref[bs].reshape(rows, LANES)
    lane = lax.broadcasted_iota(I32, (rows, LANES), 1)
    first_half = (lane & (HEAD_DIM // 2)) == 0
    lo64 = lane < HEAD_DIM

    def rope(x, cs, sn):
        partner = jnp.where(first_half, pltpu.roll(x, LANES - 32, 1), pltpu.roll(x, 32, 1))
        return x * cs + partner * sn

    def rope_wide(x, n_chunks):
        return jnp.concatenate(
            [rope(x[:, c * LANES:(c + 1) * LANES], cos, sin) for c in range(n_chunks)], axis=1)

    ps = _dot(hb, w_ref[:, 0:768])
    for j in range(6):
        for b in range(nb):
            so_ref[j, pl.ds(b0 + b, tt, stride=BATCH), :] = ps[b * tt:(b + 1) * tt, j * LANES:(j + 1) * LANES]

    pq = _dot(hb, w_ref[:, 768:1280])
    q_ref[bs] = (rope_wide(pq, 4) * (ATT_SCALE * LOG2E)).astype(BF16).reshape(nb, tt, D_ATTN)

    pkv = _dot(hb, w_ref[:, 1280:1536])
    kr = rope(pkv[:, 0:LANES], cos, sin)
    ksw = pltpu.roll(kr, HEAD_DIM, 1)
    zero = jnp.zeros_like(kr)
    kz = jnp.concatenate([jnp.where(lo64, kr, zero), jnp.where(lo64, zero, ksw),
                          jnp.where(lo64, ksw, zero), jnp.where(lo64, zero, kr)], axis=1)
    kz_ref[bs] = kz.astype(BF16).reshape(nb, tt, 4 * LANES)
    v_ref[bs] = pkv[:, LANES:2 * LANES].astype(BF16).reshape(nb, tt, KV_DIM)

    pqi = _dot(hb, w_ref[:, 1536:2048])
    qi_ref[bs] = rope_wide(pqi, 4).astype(BF16).reshape(nb, tt, IDX_HEADS * IDX_DIM)

    pk = _dot(hb, w_ref[:, 2048:N_IN_PAD])
    kir = rope(pk, jnp.where(lo64, cos, 1.0), jnp.where(lo64, sin, 0.0))
    kie = jnp.where(lo64, kir, zero)
    kiz = jnp.concatenate([kie, pltpu.roll(kie, HEAD_DIM, 1)], axis=1)
    kiz_ref[bs] = kiz.astype(BF16).reshape(nb, tt, 2 * LANES)
    wi_ref[bs] = pk.reshape(nb, tt, LANES)


def _inproj(h3, w_in_p, cos, sin, ln_g, ln_b, apply_ln, seq):
    tt = min(seq, 128)
    nt = seq // tt
    blk = lambda c: pl.BlockSpec((BATCH, tt, c), lambda i: (0, i, 0))
    out_shape = [
        jax.ShapeDtypeStruct((6, seq * BATCH, LANES), F32),
        jax.ShapeDtypeStruct((BATCH, seq, D_ATTN), BF16),
        jax.ShapeDtypeStruct((BATCH, seq, 4 * LANES), BF16),
        jax.ShapeDtypeStruct((BATCH, seq, KV_DIM), BF16),
        jax.ShapeDtypeStruct((BATCH, seq, IDX_HEADS * IDX_DIM), BF16),
        jax.ShapeDtypeStruct((BATCH, seq, 2 * LANES), BF16),
        jax.ShapeDtypeStruct((BATCH, seq, LANES), F32),
    ]
    out_specs = [pl.BlockSpec((6, tt * BATCH, LANES), lambda i: (0, i, 0)),
                 blk(D_ATTN), blk(4 * LANES), blk(KV_DIM), blk(512), blk(2 * LANES), blk(LANES)]
    if apply_ln:
        out_shape = [jax.ShapeDtypeStruct((BATCH, seq, D_MODEL), F32)] + out_shape
        out_specs = [blk(D_MODEL)] + out_specs
    return pl.pallas_call(
        functools.partial(_inproj_kernel, apply_ln, tt),
        out_shape=tuple(out_shape),
        grid=(nt,),
        in_specs=[blk(D_MODEL), _full_spec((D_MODEL, N_IN_PAD)), blk(LANES), blk(LANES),
                  _full_spec((1, D_MODEL)), _full_spec((1, D_MODEL))],
        out_specs=tuple(out_specs),
        compiler_params=pltpu.CompilerParams(dimension_semantics=("parallel",),
                                             vmem_limit_bytes=VMEM_LIMIT),
        name="inproj_ln" if apply_ln else "inproj",
    )(h3, w_in_p, cos, sin, ln_g, ln_b)


def _s5_param_kernel(lr_ref, li_ref, ls_ref, br_ref, bi_ref, a_ref, bd_ref):
    lr = lr_ref[...]
    li = li_ref[...]
    step = jnp.exp(ls_ref[...])
    mag = jnp.exp(lr * step)
    ar = mag * jnp.cos(li * step)
    ai = mag * jnp.sin(li * step)
    den = lr * lr + li * li
    nr, ni = ar - 1.0, ai
    cr = (nr * lr + ni * li) / den
    ci = (ni * lr - nr * li) / den
    br = br_ref[...]
    bi = bi_ref[...]
    bbr = cr * br - ci * bi
    bbi = cr * bi + ci * br
    row = lax.broadcasted_iota(I32, (D_S5, N_S5_STATE), 0)
    col = lax.broadcasted_iota(I32, (D_S5, N_S5_STATE), 1)
    blk = (row // S5_GROUP) == (col // S5_STATE)
    zero = jnp.zeros((D_S5, N_S5_STATE), F32)
    bd_ref[:, 0:N_S5_STATE] = jnp.where(blk, jnp.concatenate([bbr] * S5_GROUPS, axis=0), zero).astype(BF16)
    bd_ref[:, N_S5_STATE:] = jnp.where(blk, jnp.concatenate([bbi] * S5_GROUPS, axis=0), zero).astype(BF16)
    a_ref[:, 0:N_S5_STATE] = jnp.broadcast_to(ar, (SUBLANES, N_S5_STATE))
    a_ref[:, N_S5_STATE:] = jnp.broadcast_to(ai, (SUBLANES, N_S5_STATE))


def _s5_param_layout(lam_re, lam_im, log_step, b_re, b_im):
    lr = lam_re.reshape(1, N_S5_STATE)
    li = lam_im.reshape(1, N_S5_STATE)
    ls = jnp.repeat(log_step, S5_STATE).reshape(1, N_S5_STATE)
    br = b_re.transpose(2, 0, 1).reshape(S5_GROUP, N_S5_STATE)
    bi = b_im.transpose(2, 0, 1).reshape(S5_GROUP, N_S5_STATE)
    return [lr, li, ls, br, bi]


def _scan_kernel(tt, si_ref, lr_ref, li_ref, ls_ref, br_ref, bi_ref, cd_ref, dsk_ref, wglu_ref, bglu_ref,
                 cw_ref, cb_ref, wa_ref, ba_ref, wx_ref, bx_ref, lam_ref,
                 o_ref, x_ref, hs_ref, halo_ref, ab_ref, hrg_ref, y_ref, a_ref, bd_ref):
    rows = BATCH * tt
    i = pl.program_id(0)

    @pl.when(i == 0)
    def _init():
        _s5_param_kernel(lr_ref, li_ref, ls_ref, br_ref, bi_ref, a_ref, bd_ref)
        hs_ref[...] = jnp.zeros_like(hs_ref)
        halo_ref[...] = jnp.zeros_like(halo_ref)
        hrg_ref[...] = jnp.zeros_like(hrg_ref)

    u = jnp.concatenate([si_ref[0], si_ref[1]], axis=1)
    x_ref[...] = _dot(u.astype(BF16), bd_ref[...])
    half = N_S5_STATE // 2
    for hh in range(2):
        lo = hh * half
        ar = a_ref[:, lo:lo + half]
        ai = a_ref[:, N_S5_STATE + lo:N_S5_STATE + lo + half]

        def step(t, carry, lo=lo, ar=ar, ai=ai):
            hr, hi = carry
            r0 = pl.multiple_of(t * BATCH, BATCH)
            br = x_ref[pl.ds(r0, BATCH), lo:lo + half]
            bi = x_ref[pl.ds(r0, BATCH), N_S5_STATE + lo:N_S5_STATE + lo + half]
            nr = ar * hr - ai * hi + br
            ni = ar * hi + ai * hr + bi
            x_ref[pl.ds(r0, BATCH), lo:lo + half] = nr
            x_ref[pl.ds(r0, BATCH), N_S5_STATE + lo:N_S5_STATE + lo + half] = ni
            return nr, ni

        hr, hi = lax.fori_loop(
            0, tt, step,
            (hs_ref[:, lo:lo + half], hs_ref[:, N_S5_STATE + lo:N_S5_STATE + lo + half]),
            unroll=True)
        hs_ref[:, lo:lo + half] = hr
        hs_ref[:, N_S5_STATE + lo:N_S5_STATE + lo + half] = hi

    y = _dot(x_ref[...].astype(BF16), cd_ref[...]) + dsk_ref[...] * u
    y = _gelu(y)
    y = y * _sigmoid(_dot(y.astype(BF16), wglu_ref[...]) + bglu_ref[...])
    y_ref[0] = y[:, 0:LANES]
    y_ref[1] = y[:, LANES:2 * LANES]

    xr = jnp.concatenate([si_ref[2], si_ref[3]], axis=1)
    gate = jnp.concatenate([si_ref[4], si_ref[5]], axis=1)
    hal = (RG_CONV - 1) * BATCH
    xext = jnp.concatenate([halo_ref[...], xr], axis=0)
    halo_ref[...] = xr[rows - hal:rows, :]
    xc = cb_ref[...]
    for k in range(RG_CONV):
        xc = xc + cw_ref[k:k + 1, :] * xext[k * BATCH:k * BATCH + rows, :]
    xcb = xc.astype(BF16)
    r = _sigmoid(_dot(xcb, wa_ref[...]) + ba_ref[...])
    ig = _sigmoid(_dot(xcb, wx_ref[...]) + bx_ref[...])
    nl = -lam_ref[...]
    softplus = jnp.maximum(nl, 0.0) + jnp.log(1.0 + jnp.exp(-jnp.abs(nl)))
    log_a = -RG_C * r * softplus
    a = jnp.exp(log_a)
    mult = jnp.sqrt(1.0 - a * a)
    ab_ref[0] = a
    ab_ref[1] = mult * (ig * xc)

    def rg_step(t, h):
        r0 = pl.multiple_of(t * BATCH, BATCH)
        hn = ab_ref[0, pl.ds(r0, BATCH), :] * h + ab_ref[1, pl.ds(r0, BATCH), :]
        ab_ref[1, pl.ds(r0, BATCH), :] = hn
        return hn

    hrg_ref[...] = lax.fori_loop(0, tt, rg_step, hrg_ref[...], unroll=True)
    yr = ab_ref[1] * _gelu(gate)
    y_ref[2] = yr[:, 0:LANES]
    y_ref[3] = yr[:, LANES:2 * LANES]

    for b in range(BATCH):
        o_ref[b] = jnp.concatenate(
            [y_ref[j, pl.ds(b, tt, stride=BATCH), :] for j in range(4)], axis=1).astype(BF16)


def _scans(scan_in, s5p, cd, dsk, wglu, bglu, cw, cb, wa, ba, wx, bx, lam, seq):
    tt = min(seq, 128)
    nt = seq // tt
    rows = tt * BATCH
    params = list(s5p) + [cd, dsk, wglu, bglu, cw, cb, wa, ba, wx, bx, lam]
    return pl.pallas_call(
        functools.partial(_scan_kernel, tt),
        out_shape=jax.ShapeDtypeStruct((BATCH, seq, 2 * D_S5), BF16),
        grid=(nt,),
        in_specs=[pl.BlockSpec((6, rows, LANES), lambda i: (0, i, 0))] + [_full_spec(p.shape) for p in params],
        out_specs=pl.BlockSpec((BATCH, tt, 2 * D_S5), lambda i: (0, i, 0)),
        scratch_shapes=[
            pltpu.VMEM((rows, 2 * N_S5_STATE), F32),
            pltpu.VMEM((BATCH, 2 * N_S5_STATE), F32),
            pltpu.VMEM(((RG_CONV - 1) * BATCH, D_RG), F32),
            pltpu.VMEM((2, rows, D_RG), F32),
            pltpu.VMEM((BATCH, D_RG), F32),
            pltpu.VMEM((4, rows, LANES), F32),
            pltpu.VMEM((SUBLANES, 2 * N_S5_STATE), F32),
            pltpu.VMEM((D_S5, 2 * N_S5_STATE), BF16),
        ],
        compiler_params=pltpu.CompilerParams(dimension_semantics=("arbitrary",),
                                             vmem_limit_bytes=VMEM_LIMIT),
        name="scans",
    )(scan_in, *params)


def _count16(ref, skv, cand, op):
    cb = jnp.broadcast_to(cand, (PACK16, LANES))
    accs = []
    for i in range(skv // PACK16):
        one = jnp.where(op(ref[i * PACK16:(i + 1) * PACK16, :], cb), jnp.int16(1), jnp.int16(0))
        if i < 4:
            accs.append(one)
        else:
            accs[i % 4] = accs[i % 4] + one
    tot = (accs[0] + accs[1]) + (accs[2] + accs[3])
    return jnp.sum(tot.astype(I32), axis=0, keepdims=True)


def _transposed_chunks(x):
    xf = x.astype(F32)
    return jnp.concatenate([xf[:, c * LANES:(c + 1) * LANES].T for c in range(4)], axis=1).astype(BF16)


def _attn_kernel(seq, top, q_ref, kz_ref, v_ref, qi_ref, kiz_ref, wi_ref, o_ref,
                 vt_ref, sc_ref, sb_ref, lom_ref, tie_ref, bias_ref, l_ref, p_ref):
    step = pl.program_id(1)

    @pl.when(step == 0)
    def _():
        vt = v_ref[...].astype(F32).T.astype(BF16)
        ones = jnp.where(lax.broadcasted_iota(I32, (PACK16, seq), 0) == 0, 1.0, 0.0).astype(BF16)
        for g in range(N_KV_HEADS):
            vt_ref[g, 0:HEAD_DIM, :] = vt[g * HEAD_DIM:(g + 1) * HEAD_DIM, :]
            vt_ref[g, HEAD_DIM:HEAD_DIM + PACK16, :] = ones

    cls_rows = min(seq, KV_CLASS)
    blocks = cls_rows // LANES
    for c in range(seq // cls_rows):
        @pl.when(step == c)
        def _(c=c):
            skv = cls_rows * (c + 1)

            def one_block(hb, carry):
                j = step * blocks + hb
                rows = pl.ds(pl.multiple_of(hb * LANES, LANES), LANES)
                qv, qiv, wiv, ov = q_ref.at[rows], qi_ref.at[rows], wi_ref.at[rows], o_ref.at[rows]
                if skv <= top:
                    bias_ref[0:skv, :] = jnp.zeros((skv, LANES), BF16)
                    _softmax_pv(skv, j, qv, kz_ref, ov, vt_ref, bias_ref, l_ref, p_ref)
                else:
                    _attn_class(skv, top, j, qv, kz_ref, qiv, kiz_ref, wiv, ov,
                                vt_ref, sc_ref, sb_ref, lom_ref, tie_ref, bias_ref, l_ref, p_ref)
                return carry

            lax.fori_loop(0, blocks, one_block, 0)


def _attn_class(skv, top, j, q_ref, kz_ref, qi_ref, kiz_ref, wi_ref, o_ref,
                vt_ref, sc_ref, sb_ref, lom_ref, tie_ref, bias_ref, l_ref, p_ref):
    ck = ATT_CHUNK
    nc = skv // ck
    mm_chunks = [(r0, min(ATT_MM_CHUNK, skv - r0)) for r0 in range(0, skv, ATT_MM_CHUNK)]
    sub = ATT_SUB
    qb = LANES
    t_idx = j * qb + lax.broadcasted_iota(I32, (ck, qb), 1)
    s_iota = lax.broadcasted_iota(I32, (ck, qb), 0)
    t_sub = j * qb + lax.broadcasted_iota(I32, (sub, qb), 1)
    s_sub = lax.broadcasted_iota(I32, (sub, qb), 0)

    wts = wi_ref[...].T[HEAD_DIM:HEAD_DIM + IDX_HEADS, :] * IDX_SCALE
    qit = _transposed_chunks(qi_ref[...])
    for r0, mk in mm_chunks:
        se = _dot(kiz_ref[r0:r0 + mk, 0:LANES], qit)
        so = _dot(kiz_ref[r0:r0 + mk, LANES:2 * LANES], qit)
        for s0 in range(0, mk, sub):
            acc = None
            for c4 in range(4):
                te = wts[2 * c4:2 * c4 + 1, :] * jnp.maximum(se[s0:s0 + sub, c4 * qb:(c4 + 1) * qb], 0.0)
                to = wts[2 * c4 + 1:2 * c4 + 2, :] * jnp.maximum(so[s0:s0 + sub, c4 * qb:(c4 + 1) * qb], 0.0)
                acc = te + to if acc is None else acc + (te + to)
            rows = slice(r0 + s0, r0 + s0 + sub)
            sc = acc
            if r0 + s0 + sub > skv - KV_CLASS:
                sc = jnp.where((r0 + s0) + s_sub <= t_sub, acc, -jnp.inf)
            sc_ref[rows, :] = sc
            near = sc.astype(BF16)
            rounded_up = (sc - near.astype(F32)).astype(BF16) < 0
            nb = pltpu.bitcast(near, jnp.int16)
            below = pltpu.bitcast(nb + jnp.where(nb < 0, jnp.int16(1), jnp.int16(-1)), BF16)
            sb_ref[rows, :] = jnp.where(rounded_up, below, near)

    ge = lambda a, b: a >= b
    lt = lambda a, b: a < b

    def f32_of_code(u):
        k = u - HALF16
        return pltpu.bitcast(jnp.left_shift(jnp.where(k < 0, k ^ 0x7FFF, k), 16), F32)

    def coarse_body(i, tu):
        cand = jnp.minimum(tu | jnp.left_shift(jnp.int32(1), 15 - i), 65535 - NEG_INF_CODE)
        cnt = _count16(sb_ref, skv, f32_of_code(cand + NEG_INF_CODE).astype(BF16), ge)
        return jnp.where(cnt >= top, cand, tu)

    lo_code = lax.fori_loop(0, 16, coarse_body, jnp.zeros((1, qb), I32)) + NEG_INF_CODE
    lo_f = f32_of_code(lo_code)
    hi_f = f32_of_code(lo_code + 1)
    lo_b = lo_f.astype(BF16)
    hi_b = hi_f.astype(BF16)
    keep, drop = jnp.zeros((), BF16), jnp.full((), -jnp.inf, BF16)
    n_above = _count16(sb_ref, skv, hi_b, ge)
    width = hi_f - lo_f
    usable = jnp.logical_and(width > 1e-30, width < 1e30)
    scale = jnp.where(usable, 65536.0 / jnp.where(usable, width, 1.0), 0.0)
    outside = jnp.int16(-HALF16)

    def residuals(c, carry):
        r0 = pl.multiple_of(c * ck, ck)
        s = sc_ref[pl.ds(r0, ck), :]
        x = jnp.clip(jnp.where(s > lo_f, s - lo_f, 0.0) * scale, 0.0, 65534.0)
        rq = (x.astype(I32) - (HALF16 - 1)).astype(jnp.int16)
        sb = sb_ref[pl.ds(r0, ck), :]
        inside = jnp.logical_and(sb >= lo_b, sb < hi_b)
        lom_ref[pl.ds(r0, ck), :] = jnp.where(inside, rq, outside)
        return carry

    lax.fori_loop(0, nc, residuals, 0)

    def fine_body(i, tu):
        cand = tu | jnp.left_shift(jnp.int32(1), 15 - i)
        cnt = _count16(lom_ref, skv, (cand - HALF16).astype(jnp.int16), ge)
        return jnp.where(cnt >= top - n_above, cand, tu)

    cut = (lax.fori_loop(0, 16, fine_body, jnp.zeros((1, qb), I32)) - HALF16).astype(jnp.int16)

    def bias_default(c, acc):
        r0 = pl.multiple_of(c * ck, ck)
        at_cut = lom_ref[pl.ds(r0, ck), :] >= cut
        sel = jnp.logical_or(sb_ref[pl.ds(r0, ck), :] >= hi_b, at_cut)
        bias_ref[pl.ds(r0, ck), :] = jnp.where(sel, keep, drop)
        ones = jnp.where(at_cut, jnp.int16(1), jnp.int16(0))
        parts = [ones[i * PACK16:(i + 1) * PACK16, :] for i in range(ck // PACK16)]
        while len(parts) > 1:
            parts = [a + b for a, b in zip(parts[0::2], parts[1::2])]
        return acc + parts[0]

    n_cut = lax.fori_loop(0, nc, bias_default, jnp.zeros((PACK16, qb), jnp.int16))
    n_ge = n_above + jnp.sum(n_cut.astype(I32), axis=0, keepdims=True)
    tie = jnp.logical_and(n_ge > top, lo_code != NEG_INF_CODE)
    any_tie = jnp.max(jnp.where(tie, 1, 0).astype(I32)) > 0

    @pl.when(any_tie)
    def _tie():
        nbits = (skv - 1).bit_length()
        n_over = _count16(lom_ref, skv, cut, lambda a, b: a > b)
        need = top - n_above - n_over
        big = jnp.int16(2 ** nbits)

        def fill(c, carry):
            r0 = pl.multiple_of(c * ck, ck)
            idx = (r0 + s_iota).astype(jnp.int16)
            tie_ref[pl.ds(r0, ck), :] = jnp.where(lom_ref[pl.ds(r0, ck), :] == cut, idx, big)
            return carry

        lax.fori_loop(0, nc, fill, 0)

        def bit2(i, m):
            cand = m | jnp.left_shift(jnp.int32(1), nbits - 1 - i)
            cnt = _count16(tie_ref, skv, cand.astype(jnp.int16), lt)
            return jnp.where(cnt < need, cand, m)

        m = lax.fori_loop(0, nbits, bit2, jnp.zeros((1, qb), I32)).astype(jnp.int16)

        def bias_tie(c, carry):
            r0 = pl.multiple_of(c * ck, ck)
            sel = jnp.logical_or(sb_ref[pl.ds(r0, ck), :] >= hi_b,
                                 lom_ref[pl.ds(r0, ck), :] > cut)
            sel = jnp.logical_or(sel, tie_ref[pl.ds(r0, ck), :] <= m)
            bias_ref[pl.ds(r0, ck), :] = jnp.where(sel, keep, drop)
            return carry

        lax.fori_loop(0, nc, bias_tie, 0)

    _softmax_pv(skv, j, q_ref, kz_ref, o_ref, vt_ref, bias_ref, l_ref, p_ref)


def _softmax_pv(skv, j, q_ref, kz_ref, o_ref, vt_ref, bias_ref, l_ref, p_ref):
    ck = ATT_CHUNK
    nc = skv // ck
    mm_chunks = [(r0, min(ATT_MM_CHUNK, skv - r0)) for r0 in range(0, skv, ATT_MM_CHUNK)]
    sub = ATT_SUB
    qb = LANES
    t_sub = j * qb + lax.broadcasted_iota(I32, (sub, qb), 1)
    s_sub = lax.broadcasted_iota(I32, (sub, qb), 0)
    qt = _transposed_chunks(q_ref[...])
    o_parts = []
    for g in range(N_KV_HEADS):
        qet = qt[:, (2 * g) * qb:(2 * g + 2) * qb]
        m8 = jnp.full((SUBLANES, 4 * qb), -jnp.inf, F32)
        for r0, mk in mm_chunks:
            le = _dot(kz_ref[r0:r0 + mk, (2 * g) * LANES:(2 * g + 1) * LANES], qet)
            lo = _dot(kz_ref[r0:r0 + mk, (2 * g + 1) * LANES:(2 * g + 2) * LANES], qet)
            for s0 in range(0, mk, sub):
                rows = slice(r0 + s0, r0 + s0 + sub)
                b = bias_ref[rows, :].astype(F32)
                if r0 + s0 + sub > skv - KV_CLASS:
                    b = jnp.where((r0 + s0) + s_sub <= t_sub, b, -jnp.inf)
                l = jnp.concatenate([le[s0:s0 + sub] + jnp.concatenate([b, b], axis=1),
                                     lo[s0:s0 + sub] + jnp.concatenate([b, b], axis=1)], axis=1)
                l_ref[rows, :] = l
                m8 = jnp.maximum(m8, jnp.max(l.reshape(sub // SUBLANES, SUBLANES, 4 * qb), axis=0))
        mx = jnp.max(m8, axis=0, keepdims=True)

        def pbody(c, carry, mx=mx):
            r0 = pl.multiple_of(c * ck, ck)
            p_ref[pl.ds(r0, ck), :] = jnp.exp2(l_ref[pl.ds(r0, ck), :] - mx).astype(BF16)
            return carry

        lax.fori_loop(0, nc, pbody, 0)
        ot = _dot(vt_ref[g, :, 0:skv], p_ref[0:skv, :])
        o_parts.append(ot[0:HEAD_DIM, :] * (1.0 / ot[HEAD_DIM:HEAD_DIM + 1, :]))
    ot = jnp.concatenate(o_parts, axis=0)
    y = jnp.concatenate([ot[:, i * qb:(i + 1) * qb].T for i in range(4)], axis=1)
    o_ref[...] = y.astype(BF16)


def _attention(q, kz, v, qi, kiz, wi, seq, top):
    qb = LANES
    step_rows = min(seq, KV_CLASS)
    sq = pl.Squeezed()
    per_q = lambda c: pl.BlockSpec((sq, step_rows, c), lambda b, j: (b, j, 0))
    per_b = lambda c: pl.BlockSpec((sq, seq, c), lambda b, j: (b, 0, 0))
    return pl.pallas_call(
        functools.partial(_attn_kernel, seq, top),
        out_shape=jax.ShapeDtypeStruct((BATCH, seq, D_ATTN), BF16),
        grid=(BATCH, seq // step_rows),
        in_specs=[per_q(D_ATTN), per_b(4 * LANES), per_b(KV_DIM), per_q(512), per_b(2 * LANES), per_q(LANES)],
        out_specs=per_q(D_ATTN),
        scratch_shapes=[
            pltpu.VMEM((N_KV_HEADS, HEAD_DIM + PACK16, seq), BF16),
            pltpu.VMEM((seq, qb), F32),
            pltpu.VMEM((seq, qb), BF16),
            pltpu.VMEM((seq, qb), jnp.int16),
            pltpu.VMEM((seq, qb), jnp.int16),
            pltpu.VMEM((seq, qb), BF16),
            pltpu.VMEM((seq, 4 * qb), F32),
            pltpu.VMEM((seq, 4 * qb), BF16),
        ],
        compiler_params=pltpu.CompilerParams(dimension_semantics=("parallel", "arbitrary"),
                                             vmem_limit_bytes=VMEM_LIMIT),
        name="dsa_attention",
    )(q, kz, v, qi, kiz, wi)


def _mix_ffn_kernel(ys_ref, ya_ref, h_ref, p_ref, wo_ref, g1_ref, b1_ref,
                    wup_ref, wd_ref, wpg_ref, wpp_ref, g2_ref, b2_ref, o_ref):
    mix = _dot(ys_ref[...], wo_ref[0:2 * D_S5, :]) + _dot(ya_ref[...], wo_ref[2 * D_S5:, :])
    h = _ln(ALPHA * h_ref[...] + mix, g1_ref[...], b1_ref[...])
    hb = h.astype(BF16)
    gate = _dot(hb, wup_ref[:, 0:D_FF])
    up = _dot(hb, wup_ref[:, D_FF:])
    act = (gate * _sigmoid(gate) * up).astype(BF16)
    ffn = _dot(act, wd_ref[...])
    ple = _sigmoid(_dot(hb, wpg_ref[...])) * _dot(p_ref[...].astype(BF16), wpp_ref[...])
    o_ref[...] = _ln(ALPHA * h + ffn + ple, g2_ref[...], b2_ref[...])


def _mix_ffn(layer, y_scan, y_at, h, p, w_out_p, g1, b1, w_up, w_down, w_pg, w_pp, g2, b2):
    t = h.shape[0]
    tm = min(t, 512)
    row = lambda c: pl.BlockSpec((tm, c), lambda i: (i, 0))
    const = lambda shape: pl.BlockSpec(shape, lambda i: (0, 0), pipeline_mode=pl.Buffered(1))
    of_layer = lambda r, c: pl.BlockSpec((pl.Squeezed(), r, c), lambda i: (layer, 0, 0),
                                         pipeline_mode=pl.Buffered(1))
    vec = _full_spec((1, D_MODEL))
    return pl.pallas_call(
        _mix_ffn_kernel,
        out_shape=jax.ShapeDtypeStruct((t, D_MODEL), F32),
        grid=(t // tm,),
        in_specs=[row(2 * D_S5), row(D_ATTN), row(D_MODEL), pl.BlockSpec((pl.Squeezed(), tm, D_PLE), lambda i: (layer, i, 0)),
                  const((D_MIX, D_MODEL)), vec, vec,
                  of_layer(D_MODEL, 2 * D_FF), of_layer(D_FF, D_MODEL), of_layer(D_MODEL, D_MODEL),
                  of_layer(D_PLE, D_MODEL), vec, vec],
        out_specs=row(D_MODEL),
        compiler_params=pltpu.CompilerParams(dimension_semantics=("parallel",),
                                             vmem_limit_bytes=VMEM_LIMIT),
        name="mix_ffn_ln",
    )(y_scan, y_at, h, p, w_out_p, g1, b1, w_up, w_down, w_pg, w_pp, g2, b2)


def _block_diag(w):
    hh, n, m = w.shape
    eye = jnp.eye(hh, dtype=w.dtype)
    return jnp.einsum('hij,hk->hikj', w, eye).reshape(hh * n, hh * m)


def _s5_out_matrix(c_re, c_im):
    cre = _block_diag(c_re.transpose(0, 2, 1))
    cim = _block_diag(c_im.transpose(0, 2, 1))
    return jnp.concatenate([cre, -cim], axis=0).astype(BF16)


def _attn_out_rows(w_at):
    w = w_at.reshape(N_KV_HEADS, 4, HEAD_DIM, D_MODEL)
    w = w[:, jnp.array([0, 2, 1, 3])]
    return w.transpose(1, 0, 2, 3).reshape(D_ATTN, D_MODEL)


def _forward(seq, top, x, p, positions, ln_emb_g, ln_emb_b, w_in,
             s5_lam_re, s5_lam_im, s5_log_step, s5_b_re, s5_b_im, s5_c_re, s5_c_im,
             s5_d, s5_w_glu, s5_b_glu,
             rg_conv_w, rg_conv_b, rg_wa, rg_ba, rg_wx, rg_bx, rg_lam,
             w_out, ln1_g, ln1_b, ffn_w_up, ffn_w_down, ple_w_gate, ple_w_proj,
             ln2_g, ln2_b):
    t = BATCH * seq
    cos, sin = _rope_tables(positions, seq)
    row = lambda a: a.reshape(1, -1)
    p_rows = p.reshape(DEPTH, t, D_PLE)
    w_up_b, w_down_b = ffn_w_up.astype(BF16), ffn_w_down.astype(BF16)
    w_pg_b, w_pp_b = ple_w_gate.astype(BF16), ple_w_proj.astype(BF16)
    h3 = x
    for i in range(DEPTH):
        w_in_p = jnp.pad(w_in[i], ((0, 0), (0, N_IN_PAD - N_IN))).astype(BF16)
        outs = _inproj(h3, w_in_p, cos, sin, row(ln_emb_g), row(ln_emb_b), i == 0, seq)
        if i == 0:
            h3, outs = outs[0], outs[1:]
        scan_in, q, kz, v, qi, kiz, wi = outs
        s5p = _s5_param_layout(s5_lam_re[i], s5_lam_im[i], s5_log_step[i], s5_b_re[i], s5_b_im[i])
        y_scan = _scans(scan_in, s5p, _s5_out_matrix(s5_c_re[i], s5_c_im[i]), row(s5_d[i]),
                        s5_w_glu[i].astype(BF16), row(s5_b_glu[i]),
                        rg_conv_w[i], row(rg_conv_b[i]),
                        _block_diag(rg_wa[i]).astype(BF16), row(rg_ba[i]),
                        _block_diag(rg_wx[i]).astype(BF16), row(rg_bx[i]), row(rg_lam[i]), seq)
        y_at = _attention(q, kz, v, qi, kiz, wi, seq, top)
        w_out_p = jnp.concatenate([w_out[i][:2 * D_S5], _attn_out_rows(w_out[i][2 * D_S5:])], axis=0).astype(BF16)
        h2 = _mix_ffn(i, y_scan.reshape(t, 2 * D_S5), y_at.reshape(t, D_ATTN), h3.reshape(t, D_MODEL),
                      p_rows, w_out_p, row(ln1_g[i]), row(ln1_b[i]),
                      w_up_b, w_down_b, w_pg_b, w_pp_b, row(ln2_g[i]), row(ln2_b[i]))
        h3 = h2.reshape(BATCH, seq, D_MODEL)
    return h3


def kernel(x, p, positions, ln_emb_g, ln_emb_b, w_in, s5_lam_re, s5_lam_im, s5_log_step, s5_b_re, s5_b_im, s5_c_re, s5_c_im, s5_d, s5_w_glu, s5_b_glu, rg_conv_w, rg_conv_b, rg_wa, rg_ba, rg_wx, rg_bx, rg_lam, w_out, ln1_g, ln1_b, ffn_w_up, ffn_w_down, ple_w_gate, ple_w_proj, ln2_g, ln2_b):
    seq = x.shape[1]
    return _forward(seq, min(TOPK_MAX, seq // 4), x, p, positions, ln_emb_g, ln_emb_b, w_in,
                    s5_lam_re, s5_lam_im, s5_log_step, s5_b_re, s5_b_im, s5_c_re, s5_c_im,
                    s5_d, s5_w_glu, s5_b_glu,
                    rg_conv_w, rg_conv_b, rg_wa, rg_ba, rg_wx, rg_bx, rg_lam,
                    w_out, ln1_g, ln1_b, ffn_w_up, ffn_w_down, ple_w_gate, ple_w_proj,
                    ln2_g, ln2_b)
```
